```python
import math
import jax, jax.numpy as jnp
from jax import lax
import numpy as np

D_MODEL = 1024
BATCH = 32
SEQ = 2048
DEPTH = 2

GRID_W = 64
CTX_LEN = 256
D_MIX = D_MODEL
D_RWKV = D_MIX // 4
RWKV_HEAD_DIM = 64
RWKV_HEADS = D_RWKV // RWKV_HEAD_DIM
DECAY_LORA = 64
ICL_LORA = 64
D_CONV = D_MIX // 4
CONV_WIDTH = 3
D_ATTN = D_MIX // 2
DIFF_HEAD_DIM = 64
DIFF_V_DIM = 2 * DIFF_HEAD_DIM
DIFF_HEADS = D_ATTN // DIFF_V_DIM
Q_BLOCK = 128
ROPE_THETA = 10000.0
ROPE_AXIS_DIM = DIFF_HEAD_DIM // 2
NORM_EPS = 1e-6
RWKV_GN_EPS = 64e-5
IN_SPLITS = (D_RWKV, D_RWKV, D_RWKV, DECAY_LORA, DECAY_LORA, ICL_LORA, ICL_LORA, D_RWKV,
             D_CONV, D_CONV, D_CONV, D_CONV,
             D_ATTN, D_ATTN, D_ATTN, D_ATTN)
D_IN = 4 * D_RWKV + 2 * DECAY_LORA + 2 * ICL_LORA + 4 * D_CONV + 4 * D_ATTN

kernel_name = 'hybrid_rwkv7_shortconv_diffattn_prefix_dit'


def rms_norm(x, g):
    xf = x.astype(jnp.float32)
    y = xf * lax.rsqrt(jnp.mean(xf * xf, axis=-1, keepdims=True) + NORM_EPS)
    return (y * g.astype(jnp.float32)).astype(x.dtype)


def modulation(cond, mod_w, mod_b):
    m = jax.nn.silu(cond) @ mod_w + mod_b
    return jnp.split(m, 3, axis=-1)


def split_projection(u):
    idx = np.cumsum(IN_SPLITS)[:-1].tolist()
    return jnp.split(u, idx, axis=-1)


def axial_rope_tables(seq_len):
    rows = seq_len // GRID_W
    row = jnp.repeat(jnp.arange(rows, dtype=jnp.float32), GRID_W)
    col = jnp.tile(jnp.arange(GRID_W, dtype=jnp.float32), rows)
    inv_freq = ROPE_THETA ** (-jnp.arange(0, ROPE_AXIS_DIM, 2, dtype=jnp.float32) / ROPE_AXIS_DIM)
    ang_r = row[:, None] * inv_freq
    ang_c = col[:, None] * inv_freq
    return (jnp.cos(ang_r), jnp.sin(ang_r), jnp.cos(ang_c), jnp.sin(ang_c))


def rotate(x, cos, sin):
    half = x.shape[-1] // 2
    x1, x2 = x[..., :half], x[..., half:]
    return jnp.concatenate([x1 * cos - x2 * sin, x2 * cos + x1 * sin], axis=-1)


def apply_axial_rope(x, tables):
    cos_r, sin_r, cos_c, sin_c = (t[None, :, None, None, :] for t in tables)
    xf = x.astype(jnp.float32)
    out = jnp.concatenate([rotate(xf[..., :ROPE_AXIS_DIM], cos_r, sin_r),
                           rotate(xf[..., ROPE_AXIS_DIM:], cos_c, sin_c)], axis=-1)
    return out.astype(x.dtype)


def to_heads(t):
    return t.reshape(t.shape[0], t.shape[1], RWKV_HEADS, RWKV_HEAD_DIM)


def rwkv_direction_inputs(parts, d, w0, w_up, a0, a_up, k_k, k_a):
    r, k, v, lw_f, lw_b, la_f, la_b, _ = parts
    lw = lw_f if d == 0 else lw_b
    la = la_f if d == 0 else la_b
    w = -jax.nn.softplus(-(w0[d] + jnp.tanh(lw) @ w_up[d])) - 0.5
    decay = jnp.exp(-jnp.exp(w))
    a = jax.nn.sigmoid(a0[d] + la @ a_up[d])
    kk = to_heads(k * k_k)
    kk = kk / jnp.maximum(jnp.linalg.norm(kk, axis=-1, keepdims=True), 1e-12)
    k_mod = k * (1.0 + (a - 1.0) * k_a)
    return (to_heads(r), to_heads(decay), to_heads(k_mod), to_heads(v), -kk, kk * to_heads(a))


def rwkv_scan(state0, terms, reverse, with_output):
    seq_terms = terms if with_output else terms[1:]
    xs = tuple(jnp.moveaxis(t, 1, 0) for t in seq_terms)

    def step(S, inp):
        w_t, k_t, v_t, a_t, b_t = inp[-5:]
        sa = jnp.einsum('bhvk,bhk->bhv', S, a_t)
        S = S * w_t[:, :, None, :] + sa[..., None] * b_t[:, :, None, :] + v_t[..., None] * k_t[:, :, None, :]
        y_t = jnp.einsum('bhvk,bhk->bhv', S, inp[0]) if with_output else None
        return S, y_t

    S, ys = lax.scan(step, state0, xs, reverse=reverse)
    return S, (jnp.moveaxis(ys, 0, 1) if with_output else None)


def rwkv_readout(ys, ks, parts, r_k, ln_g, ln_b):
    r, v, z = to_heads(parts[0]), to_heads(parts[2]), parts[7]
    y = ys[0] + ys[1]
    mu = jnp.mean(y, axis=-1, keepdims=True)
    var = jnp.mean(jnp.square(y - mu), axis=-1, keepdims=True)
    y = (y - mu) * lax.rsqrt(var + RWKV_GN_EPS)
    B, T = y.shape[0], y.shape[1]
    y = y.reshape(B, T, D_RWKV) * ln_g + ln_b
    bonus = jnp.sum(r * (ks[0] + ks[1]) * r_k, axis=-1, keepdims=True) * v
    y = y + bonus.reshape(B, T, D_RWKV)
    return y * jax.nn.silu(z)


def rwkv_mixer(lat, ctx, w0, w_up, a0, a_up, k_k, k_a, r_k, ln_g, ln_b, need_ctx_out):
    f32 = jnp.float32
    lat = [t.astype(f32) for t in lat]
    ctx = [t.astype(f32) for t in ctx]
    w0, w_up, a0, a_up, k_k, k_a, r_k, ln_g, ln_b = (
        p.astype(f32) for p in (w0, w_up, a0, a_up, k_k, k_a, r_k, ln_g, ln_b))
    B = lat[0].shape[0]
    state0 = jnp.zeros((B, RWKV_HEADS, RWKV_HEAD_DIM, RWKV_HEAD_DIM), f32)
    ys_l, ks_l, ys_c, ks_c = [], [], [], []
    for d, reverse in ((0, False), (1, True)):
        tc = rwkv_direction_inputs(ctx, d, w0, w_up, a0, a_up, k_k, k_a)
        tl = rwkv_direction_inputs(lat, d, w0, w_up, a0, a_up, k_k, k_a)
        s_ctx, y_c = rwkv_scan(state0, tc, reverse, need_ctx_out)
        _, y_l = rwkv_scan(s_ctx, tl, reverse, True)
        ys_l.append(y_l)
        ks_l.append(tl[2])
        ys_c.append(y_c)
        ks_c.append(tc[2])
    out_l = rwkv_readout(ys_l, ks_l, lat, r_k, ln_g, ln_b)
    out_c = rwkv_readout(ys_c, ks_c, ctx, r_k, ln_g, ln_b) if need_ctx_out else None
    return out_l, out_c


def short_conv_mixer(b_gate, c_gate, h, z, conv_w):
    u = c_gate * h
    T = u.shape[1]
    up = jnp.pad(u, ((0, 0), (CONV_WIDTH // 2, CONV_WIDTH // 2), (0, 0)))
    y = up[:, 0:T] * conv_w[0]
    for j in range(1, CONV_WIDTH):
        y = y + up[:, j:j + T] * conv_w[j]
    return b_gate * y * jax.nn.silu(z)


def diff_attend(q, k, v, lam, scale):
    s = jnp.einsum('bqhjd,bkhjd->bhjqk', q, k).astype(jnp.float32) * scale
    p = jax.nn.softmax(s, axis=-1)
    attn = p[:, :, 0] - lam * p[:, :, 1]
    return jnp.einsum('bhqk,bkhe->bqhe', attn.astype(v.dtype), v)


def diff_attention_mixer(lat, ctx, diff_lambda, subln_g, rope, layer_idx, need_ctx_out):
    q_l, k_l, v_l, z_l = lat
    q_c, k_c, v_c, z_c = ctx
    B, T = q_l.shape[0], q_l.shape[1]

    def qk_heads(t):
        return t.reshape(t.shape[0], t.shape[1], DIFF_HEADS, 2, DIFF_HEAD_DIM)

    def v_heads(t):
        return t.reshape(t.shape[0], t.shape[1], DIFF_HEADS, DIFF_V_DIM)

    q_l = apply_axial_rope(qk_heads(q_l), rope)
    k_l = apply_axial_rope(qk_heads(k_l), rope)
    v_l = v_heads(v_l)
    q_c, k_c, v_c = qk_heads(q_c), qk_heads(k_c), v_heads(v_c)

    lam_init = 0.8 - 0.6 * math.exp(-0.3 * layer_idx)
    lf = diff_lambda.astype(jnp.float32)
    lam = jnp.exp(jnp.sum(lf[0] * lf[1])) - jnp.exp(jnp.sum(lf[2] * lf[3])) + lam_init
    scale = DIFF_HEAD_DIM ** -0.5

    def finish(o, z):
        o = rms_norm(o, subln_g) * (1.0 - lam_init)
        return o.reshape(o.shape[0], o.shape[1], D_ATTN) * jax.nn.silu(z)

    k_all = jnp.concatenate([k_l, k_c], axis=1)
    v_all = jnp.concatenate([v_l, v_c], axis=1)
    nb = T // Q_BLOCK
    qb = jnp.moveaxis(q_l.reshape(B, nb, Q_BLOCK, DIFF_HEADS, 2, DIFF_HEAD_DIM), 1, 0)
    o = lax.map(lambda qblk: diff_attend(qblk, k_all, v_all, lam, scale), qb)
    o = jnp.moveaxis(o, 0, 1).reshape(B, T, DIFF_HEADS, DIFF_V_DIM)
    out_l = finish(o, z_l)
    out_c = finish(diff_attend(q_c, k_c, v_c, lam, scale), z_c) if need_ctx_out else None
    return out_l, out_c


def hybrid_layer(x, xc, c, c_ctx, mod_w, mod_b, pre_g, post_g, w_in, w_out,
                 rwkv_w0, rwkv_w_up, rwkv_a0, rwkv_a_up, rwkv_k_k, rwkv_k_a, rwkv_r_k,
                 rwkv_ln_g, rwkv_ln_b, conv_w, diff_lambda, diff_subln_g,
                 rope, layer_idx, need_ctx_out):
    shift, scale, gate = modulation(c, mod_w, mod_b)
    shift_c, scale_c, gate_c = modulation(c_ctx, mod_w, mod_b)
    h = rms_norm(x, pre_g) * (1.0 + scale[:, None, :]) + shift[:, None, :]
    hc = rms_norm(xc, pre_g) * (1.0 + scale_c) + shift_c
    p_l = split_projection(h @ w_in)
    p_c = split_projection(hc @ w_in)

    y_rwkv_l, y_rwkv_c = rwkv_mixer(p_l[0:8], p_c[0:8], rwkv_w0, rwkv_w_up, rwkv_a0, rwkv_a_up,
                                    rwkv_k_k, rwkv_k_a, rwkv_r_k, rwkv_ln_g, rwkv_ln_b, need_ctx_out)
    y_conv_l = short_conv_mixer(*p_l[8:12], conv_w)
    y_attn_l, y_attn_c = diff_attention_mixer(p_l[12:16], p_c[12:16], diff_lambda, diff_subln_g,
                                              rope, layer_idx, need_ctx_out)

    y_l = jnp.concatenate([y_rwkv_l.astype(x.dtype), y_conv_l, y_attn_l], axis=-1) @ w_out
    x = x + gate[:, None, :] * rms_norm(y_l, post_g)
    if need_ctx_out:
        y_conv_c = short_conv_mixer(*p_c[8:12], conv_w)
        y_c = jnp.concatenate([y_rwkv_c.astype(xc.dtype), y_conv_c, y_attn_c], axis=-1) @ w_out
        xc = xc + gate_c * rms_norm(y_c, post_g)
    return x, xc


def setup_inputs(seed: int = 0) -> dict:
    key = jax.random.key(seed)
    ks = jax.random.split(key, 24)
    f32 = jnp.float32
    L = DEPTH

    def nrm(k, shape, s):
        return jax.random.normal(k, shape, f32) * s

    return {
        'x': nrm(ks[0], (BATCH, SEQ, D_MODEL), 1.0),
        'c': nrm(ks[1], (BATCH, D_MODEL), 1.0),
        'ctx': nrm(ks[2], (BATCH, CTX_LEN, D_MODEL), 1.0),
        'c_ctx': nrm(ks[3], (D_MODEL,), 1.0),
        'mod_w': nrm(ks[4], (L, D_MODEL, 3 * D_MODEL), 0.5 * D_MODEL ** -0.5),
        'mod_b': nrm(ks[5], (L, 3 * D_MODEL), 0.01),
        'norm_pre_g': 1.0 + nrm(ks[6], (L, D_MODEL), 0.05),
        'norm_post_g': 1.0 + nrm(ks[7], (L, D_MODEL), 0.05),
        'w_in': nrm(ks[8], (L, D_MODEL, D_IN), D_MODEL ** -0.5),
        'w_out': nrm(ks[9], (L, D_MIX, D_MODEL), D_MIX ** -0.5),
        'rwkv_w0': jax.random.uniform(ks[10], (L, 2, D_RWKV), f32, -6.0, -1.0),
        'rwkv_w_up': nrm(ks[11], (L, 2, DECAY_LORA, D_RWKV), 0.05),
        'rwkv_a0': nrm(ks[12], (L, 2, D_RWKV), 0.5),
        'rwkv_a_up': nrm(ks[13], (L, 2, ICL_LORA, D_RWKV), 0.3 * ICL_LORA ** -0.5),
        'rwkv_k_k': 0.85 + nrm(ks[14], (L, D_RWKV), 0.05),
        'rwkv_k_a': 1.0 + nrm(ks[15], (L, D_RWKV), 0.05),
        'rwkv_r_k': nrm(ks[16], (L, RWKV_HEADS, RWKV_HEAD_DIM), 0.1),
        'rwkv_ln_g': 1.0 + nrm(ks[17], (L, D_RWKV), 0.05),
        'rwkv_ln_b': nrm(ks[18], (L, D_RWKV), 0.01),
        'conv_w': nrm(ks[19], (L, CONV_WIDTH, D_CONV), CONV_WIDTH ** -0.5),
        'diff_lambda': nrm(ks[20], (L, 4, DIFF_HEAD_DIM), 0.1),
        'diff_subln_g': 1.0 + nrm(ks[21], (L, DIFF_V_DIM), 0.05),
    }


def reference(x, c, ctx, c_ctx, mod_w, mod_b, norm_pre_g, norm_post_g, w_in, w_out,
              rwkv_w0, rwkv_w_up, rwkv_a0, rwkv_a_up, rwkv_k_k, rwkv_k_a, rwkv_r_k,
              rwkv_ln_g, rwkv_ln_b, conv_w, diff_lambda, diff_subln_g):
    rope = axial_rope_tables(x.shape[1])
    xc = ctx
    for l in range(DEPTH):
        x, xc = hybrid_layer(x, xc, c, c_ctx, mod_w[l], mod_b[l], norm_pre_g[l], norm_post_g[l],
                             w_in[l], w_out[l], rwkv_w0[l], rwkv_w_up[l], rwkv_a0[l], rwkv_a_up[l],
                             rwkv_k_k[l], rwkv_k_a[l], rwkv_r_k[l], rwkv_ln_g[l], rwkv_ln_b[l],
                             conv_w[l], diff_lambda[l], diff_subln_g[l],
                             rope, l, l < DEPTH - 1)
    return x
```

```python
import functools
import math

import jax
import jax.numpy as jnp
from jax import lax
from jax.experimental import pallas as pl
from jax.experimental.pallas import tpu as pltpu

F32 = jnp.float32
BF16 = jnp.bfloat16

D_MODEL = 1024
D_RWKV = 256
RWKV_HEAD = 64
LORA = 64
D_CONV = 256
CONV_WIDTH = 3
D_ATTN = 512
DIFF_HEAD_DIM = 64
DIFF_V_DIM = 128
DIFF_HEADS = 4
GRID_W = 64
ROPE_THETA = 10000.0
ROPE_AXIS_DIM = 32
NORM_EPS = 1e-6
RWKV_GN_EPS = 64e-5
D_IN = 4352
C_RWKV, C_CONV, C_Q, C_K, C_V, C_ZA = 0, 1280, 2304, 2816, 3328, 3840
N_RWKV = 1280
N_CONV = 1024

CHUNK = 64
PAIR = 128
VMEM_LIMIT = 56 * 1024 * 1024

NT_DIMS = (((1,), (1,)), ((), ()))
TN_DIMS = (((0,), (0,)), ((), ()))


def _cparams(sem):
    return pltpu.CompilerParams(dimension_semantics=sem, vmem_limit_bytes=VMEM_LIMIT)


def _silu(x):
    return x * jax.nn.sigmoid(x)


def _split3(x):
    hi = x.astype(BF16)
    r1 = x - hi.astype(F32)
    mid = r1.astype(BF16)
    lo = (r1 - mid.astype(F32)).astype(BF16)
    return hi, mid, lo


def _dot_exact_rhs(x, m):
    hi, mid, lo = _split3(x)
    return (jnp.dot(hi, m, preferred_element_type=F32)
            + jnp.dot(mid, m, preferred_element_type=F32)
            + jnp.dot(lo, m, preferred_element_type=F32))


def _mod_kernel(c_ref, w_ref, b_ref, o_ref):
    a = _silu(c_ref[...])
    o_ref[0] = jnp.dot(a, w_ref[0], preferred_element_type=F32,
                       precision=lax.Precision.HIGHEST) + b_ref[0]


def _modulation(cond, mod_w, mod_b):
    L = mod_w.shape[0]
    R = cond.shape[0]
    tn = 512
    return pl.pallas_call(
        _mod_kernel,
        grid=(L, 3 * D_MODEL // tn),
        in_specs=[pl.BlockSpec((R, D_MODEL), lambda l, j: (0, 0)),
                  pl.BlockSpec((1, D_MODEL, tn), lambda l, j: (l, 0, j)),
                  pl.BlockSpec((1, 1, tn), lambda l, j: (l, 0, j))],
        out_specs=pl.BlockSpec((1, R, tn), lambda l, j: (l, 0, j)),
        out_shape=jax.ShapeDtypeStruct((L, R, 3 * D_MODEL), F32),
        compiler_params=_cparams(("parallel", "parallel")),
        name="modulation",
    )(cond, mod_w, mod_b.reshape(L, 1, 3 * D_MODEL))


def _rope(t, cosv, sinv):
    lane = lax.broadcasted_iota(jnp.int32, t.shape, 1)
    first_half = (lane % 32) < 16
    partner = jnp.where(first_half, pltpu.roll(t, 112, 1), pltpu.roll(t, 16, 1))
    return t * cosv + partner * sinv


def _inproj_kernel(x_ref, mod_ref, g_ref, w_ref, cos_ref, sin_ref,
                   prw_ref, pcv_ref, q_ref, k_ref, v_ref, za_ref, *, rope):
    xf = x_ref[...]
    ms = jnp.mean(xf * xf, axis=-1, keepdims=True)
    y = xf * lax.rsqrt(ms + NORM_EPS) * g_ref[...]
    shift = mod_ref[0, 0:1, :]
    scale = mod_ref[0, 1:2, :]
    h = (y * (1.0 + scale) + shift).astype(BF16)

    def proj(c0, width):
        return jnp.dot(h, w_ref[:, c0:c0 + width], preferred_element_type=F32)

    for c in range(0, N_RWKV, 256):
        prw_ref[:, c:c + 256] = proj(C_RWKV + c, 256).astype(BF16)
    for c in range(0, N_CONV, 256):
        pcv_ref[:, c:c + 256] = proj(C_CONV + c, 256).astype(BF16)
    for c in range(0, D_ATTN, 256):
        v_ref[:, c:c + 256] = proj(C_V + c, 256).astype(BF16)
        za_ref[:, c:c + 256] = proj(C_ZA + c, 256).astype(BF16)
    qk_scale = DIFF_HEAD_DIM ** -0.5
    if rope:
        cosv = cos_ref[...]
        sinv = sin_ref[...]
    for c in range(0, D_ATTN, 128):
        tq = proj(C_Q + c, 128)
        tk = proj(C_K + c, 128)
        if rope:
            tq = _rope(tq, cosv, sinv)
            tk = _rope(tk, cosv, sinv)
        q_ref[:, c:c + 128] = (tq * qk_scale).astype(BF16)
        k_ref[:, c:c + 128] = tk.astype(BF16)


def _inproj(x2d, mod_l, mod_row_of_block, pre_g, w_in_bf, cos_t, sin_t, *, tm, rope, seq_blocks):
    M = x2d.shape[0]
    outs = [jax.ShapeDtypeStruct((M, n), BF16) for n in (N_RWKV, N_CONV, D_ATTN, D_ATTN, D_ATTN, D_ATTN)]
    row = lambda n: pl.BlockSpec((tm, n), lambda i: (i, 0))
    return pl.pallas_call(
        functools.partial(_inproj_kernel, rope=rope),
        grid=(M // tm,),
        in_specs=[row(D_MODEL),
                  pl.BlockSpec((1, 3, D_MODEL), lambda i: (mod_row_of_block(i), 0, 0)),
                  pl.BlockSpec((1, D_MODEL), lambda i: (0, 0)),
                  pl.BlockSpec((D_MODEL, D_IN), lambda i: (0, 0)),
                  pl.BlockSpec((tm, 128), lambda i: (i % seq_blocks, 0)),
                  pl.BlockSpec((tm, 128), lambda i: (i % seq_blocks, 0))],
        out_specs=[row(N_RWKV), row(N_CONV), row(D_ATTN), row(D_ATTN), row(D_ATTN), row(D_ATTN)],
        out_shape=outs,
        compiler_params=_cparams(("parallel",)),
        name="inproj_rope" if rope else "inproj",
    )(x2d, mod_l, pre_g, w_in_bf, cos_t, sin_t)


def _tri_mask(reverse, inclusive):
    t = lax.broadcasted_iota(jnp.int32, (PAIR, PAIR), 0) % CHUNK
    s = lax.broadcasted_iota(jnp.int32, (PAIR, PAIR), 1) % CHUNK
    if reverse:
        return (s >= t) if inclusive else (s > t)
    return (s <= t) if inclusive else (s < t)


def _expand_pair(x, p):
    xs = x[:, PAIR * p:PAIR * (p + 1)]
    lane = lax.broadcasted_iota(jnp.int32, xs.shape, 1)
    zero = jnp.zeros_like(xs)
    return jnp.concatenate([jnp.where(lane < RWKV_HEAD, xs, zero),
                            jnp.where(lane >= RWKV_HEAD, xs, zero)], axis=0)


def _rwkv_chunk(X, prm, d, reverse, states, with_output):
    w0, wup, a0, aup, k_k, k_a, ones_bd = prm
    r = X[:, 0:256].astype(F32)
    k = X[:, 256:512].astype(F32)
    v = X[:, 512:768]
    lw = X[:, 768:896]
    la = X[:, 896:1024]
    wraw = w0[d:d + 1, :] + jnp.dot(jnp.tanh(lw.astype(F32)).astype(BF16), wup[d],
                                    preferred_element_type=F32)
    wlog = -jnp.exp(-jax.nn.softplus(-wraw) - 0.5)
    a = jax.nn.sigmoid(a0[d:d + 1, :] + jnp.dot(la, aup[d], preferred_element_type=F32))
    kkr = k * k_k
    ss = _dot_exact_rhs(kkr * kkr, ones_bd)
    kk = kkr * lax.rsqrt(jnp.maximum(ss, 1e-24))
    kmod = k * (1.0 + (a - 1.0) * k_a)
    bb = kk * a
    ti = lax.broadcasted_iota(jnp.int32, (CHUNK, CHUNK), 0)
    si = lax.broadcasted_iota(jnp.int32, (CHUNK, CHUNK), 1)
    ltri = jnp.where((si >= ti) if reverse else (si <= ti), 1.0, 0.0).astype(BF16)
    hi, mid, lo = _split3(wlog)
    g = (jnp.dot(ltri, hi, preferred_element_type=F32) + jnp.dot(ltri, mid, preferred_element_type=F32)
         + jnp.dot(ltri, lo, preferred_element_type=F32))
    g_end = g[0:1, :] if reverse else g[CHUNK - 1:CHUNK, :]
    eg = jnp.exp(g)
    eneg = jnp.exp(-g)
    egm = jnp.exp(g - wlog)
    ec = jnp.exp(g_end - g)
    at = (-kk * egm).astype(BF16)
    bt = (bb * eneg).astype(BF16)
    kt = (kmod * eneg).astype(BF16)
    bh = (bb * ec).astype(BF16)
    kh = (kmod * ec).astype(BF16)
    rt = (r * eg).astype(BF16) if with_output else None
    decay_end = jnp.exp(g_end)

    strict = _tri_mask(reverse, False)
    incl = _tri_mask(reverse, True)
    eye = (lax.broadcasted_iota(jnp.int32, (PAIR, PAIR), 0)
           == lax.broadcasted_iota(jnp.int32, (PAIR, PAIR), 1))
    ys = []
    new_states = []
    for p in range(2):
        Pa = _expand_pair(at, p)
        Pb = _expand_pair(bt, p)
        Pk = _expand_pair(kt, p)
        Bh = _expand_pair(bh, p)
        Kh = _expand_pair(kh, p)
        Vx = _expand_pair(v, p)
        rhs = jnp.concatenate([Pb, Pk], axis=0)
        if with_output:
            Pr = _expand_pair(rt, p)
            A4 = lax.dot_general(jnp.concatenate([Pa, Pr], axis=0), rhs, NT_DIMS,
                                 preferred_element_type=F32)
            Arb = jnp.where(incl, A4[PAIR:, :PAIR], 0.0)
            Ark = jnp.where(incl, A4[PAIR:, PAIR:], 0.0)
        else:
            A4 = lax.dot_general(Pa, rhs, NT_DIMS, preferred_element_type=F32)
        Aab = jnp.where(strict, A4[:PAIR, :PAIR], 0.0)
        Aak = jnp.where(strict, A4[:PAIR, PAIR:], 0.0)
        Tm = jnp.where(eye, 1.0, 0.0) + Aab
        Pw = Aab
        for _ in range(5):
            Pwb = Pw.astype(BF16)
            Pw = jnp.dot(Pwb, Pwb, preferred_element_type=F32)
            Tm = Tm + jnp.dot(Tm.astype(BF16), Pw.astype(BF16), preferred_element_type=F32)
        AkV = jnp.dot(Aak.astype(BF16), Vx, preferred_element_type=F32)
        Z = jnp.concatenate([Pa, AkV.astype(BF16)], axis=1)
        WU = jnp.dot(Tm.astype(BF16), Z, preferred_element_type=F32)
        R2 = jnp.concatenate(
            [WU.astype(BF16), jnp.concatenate([jnp.zeros_like(Vx), Vx], axis=1)], axis=0)
        MH = lax.dot_general(jnp.concatenate([Bh, Kh], axis=0), R2, TN_DIMS,
                             preferred_element_type=F32)
        H = states[p]
        Hb = H.astype(BF16)
        gcol = jnp.broadcast_to(decay_end[:, PAIR * p:PAIR * (p + 1)], (PAIR, PAIR)).T
        new_states.append(gcol * H + jnp.dot(MH[:, :PAIR].astype(BF16), Hb, preferred_element_type=F32)
                          + MH[:, PAIR:])
        if with_output:
            QY = jnp.dot(jnp.concatenate([Arb, Ark], axis=1).astype(BF16), R2,
                         preferred_element_type=F32)
            Qh = Pr.astype(F32) + QY[:, :PAIR]
            yx = jnp.dot(Qh.astype(BF16), Hb, preferred_element_type=F32) + QY[:, PAIR:]
            ys.append(yx[:CHUNK] + yx[CHUNK:])
    y = jnp.concatenate(ys, axis=1) if with_output else None
    return y, new_states


def _rwkv_kernel(*refs, bb, with_output, has_init):
    it = iter(refs)
    pf_ref = next(it)
    pb_ref = next(it)
    w0_ref, wup_ref, a0_ref, aup_ref, kk_ref, ka_ref, ones_ref = (next(it) for _ in range(7))
    s0_ref = next(it) if has_init else None
    if with_output:
        yf_ref = next(it)
        yb_ref = next(it)
    st_ref = next(it)
    j = pl.program_id(1)

    @pl.when(j == 0)
    def _():
        if has_init:
            st_ref[...] = s0_ref[...]
        else:
            st_ref[...] = jnp.zeros_like(st_ref)

    prm = (w0_ref[...], wup_ref[...], a0_ref[...], aup_ref[...], kk_ref[...], ka_ref[...], ones_ref[...])
    for b in range(bb):
        for d, (p_ref, reverse) in enumerate(((pf_ref, False), (pb_ref, True))):
            states = [st_ref[b, 2 * d], st_ref[b, 2 * d + 1]]
            y, new_states = _rwkv_chunk(p_ref[b], prm, d, reverse, states, with_output)
            st_ref[b, 2 * d] = new_states[0]
            st_ref[b, 2 * d + 1] = new_states[1]
            if with_output:
                (yb_ref if reverse else yf_ref)[b] = y.astype(BF16)


def _rwkv_scan(p_rwkv, prm, init_state, *, with_output, bb):
    B, T, _ = p_rwkv.shape
    n = T // CHUNK
    has_init = init_state is not None
    full = lambda a: pl.BlockSpec(a.shape, lambda i, j: (0,) * a.ndim)
    tok = lambda width, rev: pl.BlockSpec(
        (bb, CHUNK, width), (lambda i, j: (i, n - 1 - j, 0)) if rev else (lambda i, j: (i, j, 0)))
    st_spec = pl.BlockSpec((bb, 4, PAIR, PAIR), lambda i, j: (i, 0, 0, 0))
    in_specs = [tok(N_RWKV, False), tok(N_RWKV, True)] + [full(a) for a in prm]
    args = [p_rwkv, p_rwkv] + list(prm)
    if has_init:
        in_specs.append(st_spec)
        args.append(init_state)
    out_specs = []
    out_shape = []
    if with_output:
        out_specs += [tok(D_RWKV, False), tok(D_RWKV, True)]
        out_shape += [jax.ShapeDtypeStruct((B, T, D_RWKV), BF16)] * 2
    out_specs.append(st_spec)
    out_shape.append(jax.ShapeDtypeStruct((B, 4, PAIR, PAIR), F32))
    res = pl.pallas_call(
        functools.partial(_rwkv_kernel, bb=bb, with_output=with_output, has_init=has_init),
        grid=(B // bb, n),
        in_specs=in_specs,
        out_specs=out_specs,
        out_shape=out_shape,
        compiler_params=_cparams(("parallel", "arbitrary")),
        name="rwkv_scan" if with_output else "rwkv_state",
    )(*args)
    if with_output:
        return res[0], res[1], res[2]
    return None, None, res[0]


def _mixpost_kernel(yf_ref, yb_ref, prw_ref, pcv_ref, a0_ref, aup_ref, ka_ref, rk_ref,
                    lng_ref, lnb_ref, cw_ref, ones_ref, o_ref, *, rows):
    T = o_ref.shape[1]
    ones_bd = ones_ref[...]
    mean_bd = ones_bd * (1.0 / RWKV_HEAD)
    for c0 in range(0, T, rows):
        sl = slice(c0, c0 + rows)
        y = yf_ref[0, sl, :].astype(F32) + yb_ref[0, sl, :].astype(F32)
        mu = _dot_exact_rhs(y, mean_bd)
        yc = y - mu
        var = _dot_exact_rhs(yc * yc, mean_bd)
        yn = yc * lax.rsqrt(var + RWKV_GN_EPS) * lng_ref[...] + lnb_ref[...]
        r = prw_ref[0, sl, 0:256].astype(F32)
        k = prw_ref[0, sl, 256:512].astype(F32)
        v = prw_ref[0, sl, 512:768].astype(F32)
        la = prw_ref[0, sl, 896:1024]
        z = prw_ref[0, sl, 1024:1280].astype(F32)
        a_f = jax.nn.sigmoid(a0_ref[0:1, :] + jnp.dot(la, aup_ref[0], preferred_element_type=F32))
        a_b = jax.nn.sigmoid(a0_ref[1:2, :] + jnp.dot(la, aup_ref[1], preferred_element_type=F32))
        ksum = k * (2.0 + (a_f + a_b - 2.0) * ka_ref[...])
        bonus = _dot_exact_rhs(r * ksum * rk_ref[...], ones_bd) * v
        o_ref[0, sl, 0:256] = ((yn + bonus) * _silu(z)).astype(BF16)
        lo = max(c0 - 8, 0)
        hi = min(c0 + rows + 8, T)
        u = pcv_ref[0, lo:hi, 256:512].astype(F32) * pcv_ref[0, lo:hi, 512:768].astype(F32)
        n_u = hi - lo
        off = c0 - lo
        trow = lax.broadcasted_iota(jnp.int32, (n_u, D_CONV), 0) + lo
        u_prev = jnp.where(trow == 0, 0.0, pltpu.roll(u, 1, 0))
        u_next = jnp.where(trow == T - 1, 0.0, pltpu.roll(u, n_u - 1, 0))
        conv = (u_prev * cw_ref[0:1, :] + u * cw_ref[1:2, :] + u_next * cw_ref[2:3, :])[off:off + rows]
        bg = pcv_ref[0, sl, 0:256].astype(F32)
        zc = pcv_ref[0, sl, 768:1024].astype(F32)
        o_ref[0, sl, 256:512] = (bg * conv * _silu(zc)).astype(BF16)


def _mixpost(yf, yb, p_rwkv, p_conv, prm):
    B, T, _ = p_rwkv.shape
    rows = min(T, 256)
    full = lambda a: pl.BlockSpec(a.shape, lambda b: (0,) * a.ndim)
    seq = lambda n: pl.BlockSpec((1, T, n), lambda b: (b, 0, 0))
    return pl.pallas_call(
        functools.partial(_mixpost_kernel, rows=rows),
        grid=(B,),
        in_specs=[seq(D_RWKV), seq(D_RWKV), seq(N_RWKV), seq(N_CONV)] + [full(a) for a in prm],
        out_specs=seq(D_RWKV + D_CONV),
        out_shape=jax.ShapeDtypeStruct((B, T, D_RWKV + D_CONV), BF16),
        compiler_params=_cparams(("parallel",)),
        name="mixpost",
    )(yf, yb, p_rwkv, p_conv, *prm)


def _attn_kernel(*refs, n_kv, lam_init):
    dl_ref, g_ref, q_ref, z_ref = refs[:4]
    kv_refs = refs[4:4 + 2 * n_kv]
    o_ref = refs[4 + 2 * n_kv]
    q = q_ref[0]
    lane = lax.broadcasted_iota(jnp.int32, q.shape, 1)
    zero = jnp.zeros_like(q)
    ks = [kv_refs[2 * i][0] for i in range(n_kv)]
    vs = [kv_refs[2 * i + 1][0] for i in range(n_kv)]

    def softmax_parts(qm):
        s = [lax.dot_general(qm, kk, NT_DIMS, preferred_element_type=F32) for kk in ks]
        m = functools.reduce(jnp.maximum, [jnp.max(x, axis=-1, keepdims=True) for x in s])
        p = [jnp.exp(x - m) for x in s]
        l = functools.reduce(lambda a, b: a + b, [jnp.sum(x, axis=-1, keepdims=True) for x in p])
        return p, l

    p0, l0 = softmax_parts(jnp.where(lane < DIFF_HEAD_DIM, q, zero))
    p1, l1 = softmax_parts(jnp.where(lane >= DIFF_HEAD_DIM, q, zero))
    dl = dl_ref[...]
    lam = (jnp.exp(jnp.sum(dl[0:1] * dl[1:2], axis=-1, keepdims=True))
           - jnp.exp(jnp.sum(dl[2:3] * dl[3:4], axis=-1, keepdims=True)) + lam_init)
    w0 = 1.0 / l0
    w1 = lam / l1
    o = None
    for a0, a1, vv in zip(p0, p1, vs):
        part = jnp.dot((a0 * w0 - a1 * w1).astype(BF16), vv, preferred_element_type=F32)
        o = part if o is None else o + part
    ms = jnp.mean(o * o, axis=-1, keepdims=True)
    o = o * lax.rsqrt(ms + NORM_EPS) * g_ref[...] * (1.0 - lam_init)
    o_ref[0] = (o * _silu(z_ref[0].astype(F32))).astype(BF16)


def _attention(q, z, kvs, diff_lambda, subln_g, layer_idx, *, tq):
    B, Tq, _ = q.shape
    lam_init = 0.8 - 0.6 * math.exp(-0.3 * layer_idx)
    qspec = pl.BlockSpec((1, tq, DIFF_V_DIM), lambda b, h, i: (b, i, h))
    in_specs = [pl.BlockSpec(diff_lambda.shape, lambda b, h, i: (0, 0)),
                pl.BlockSpec((1, DIFF_V_DIM), lambda b, h, i: (0, 0)),
                qspec, qspec]
    args = [diff_lambda, subln_g, q, z]
    for k, v in kvs:
        Tk = k.shape[1]
        kvspec = pl.BlockSpec((1, Tk, DIFF_V_DIM), lambda b, h, i: (b, 0, h))
        in_specs += [kvspec, kvspec]
        args += [k, v]
    return pl.pallas_call(
        functools.partial(_attn_kernel, n_kv=len(kvs), lam_init=lam_init),
        grid=(B, DIFF_HEADS, Tq // tq),
        in_specs=in_specs,
        out_specs=qspec,
        out_shape=jax.ShapeDtypeStruct((B, Tq, D_ATTN), BF16),
        compiler_params=_cparams(("parallel", "parallel", "arbitrary")),
        name="diff_attn",
    )(*args)


def _outproj_kernel(y01_ref, ya_ref, w_ref, x_ref, mod_ref, g_ref, o_ref):
    half = D_RWKV + D_CONV
    acc = (jnp.dot(y01_ref[...], w_ref[0:half, :], preferred_element_type=F32)
           + jnp.dot(ya_ref[...], w_ref[half:, :], preferred_element_type=F32))
    ms = jnp.mean(acc * acc, axis=-1, keepdims=True)
    yn = acc * lax.rsqrt(ms + NORM_EPS) * g_ref[...]
    o_ref[...] = x_ref[...] + mod_ref[0, 2:3, :] * yn


def _outproj(y01, ya, w_out_bf, x2d, mod_l, mod_row_of_block, post_g, *, tm):
    M = x2d.shape[0]
    row = lambda n: pl.BlockSpec((tm, n), lambda i: (i, 0))
    return pl.pallas_call(
        _outproj_kernel,
        grid=(M // tm,),
        in_specs=[row(D_RWKV + D_CONV), row(D_ATTN),
                  pl.BlockSpec((D_MODEL, D_MODEL), lambda i: (0, 0)),
                  row(D_MODEL),
                  pl.BlockSpec((1, 3, D_MODEL), lambda i: (mod_row_of_block(i), 0, 0)),
                  pl.BlockSpec((1, D_MODEL), lambda i: (0, 0))],
        out_specs=row(D_MODEL),
        out_shape=jax.ShapeDtypeStruct((M, D_MODEL), F32),
        compiler_params=_cparams(("parallel",)),
        name="outproj",
    )(y01, ya, w_out_bf, x2d, mod_l, post_g)


def _rope_tables(T):
    rows = T // GRID_W
    row = jnp.repeat(jnp.arange(rows, dtype=F32), GRID_W)
    col = jnp.tile(jnp.arange(GRID_W, dtype=F32), rows)
    inv_freq = ROPE_THETA ** (-jnp.arange(0, ROPE_AXIS_DIM, 2, dtype=F32) / ROPE_AXIS_DIM)
    ang_r = row[:, None] * inv_freq
    ang_c = col[:, None] * inv_freq
    cr, sr, cc, sc = jnp.cos(ang_r), jnp.sin(ang_r), jnp.cos(ang_c), jnp.sin(ang_c)
    cos64 = jnp.concatenate([cr, cr, cc, cc], axis=-1)
    sin64 = jnp.concatenate([-sr, sr, -sc, sc], axis=-1)
    return jnp.tile(cos64, (1, 2)), jnp.tile(sin64, (1, 2))


def _pad_lora(w_up):
    z = jnp.zeros_like(w_up[0])
    return jnp.stack([jnp.concatenate([w_up[0], z], axis=0),
                      jnp.concatenate([z, w_up[1]], axis=0)]).astype(BF16)


def kernel(x, c, ctx, c_ctx, mod_w, mod_b, norm_pre_g, norm_post_g, w_in, w_out, rwkv_w0, rwkv_w_up,
           rwkv_a0, rwkv_a_up, rwkv_k_k, rwkv_k_a, rwkv_r_k, rwkv_ln_g, rwkv_ln_b, conv_w, diff_lambda,
           diff_subln_g):
    B, T, D = x.shape
    Tc = ctx.shape[1]
    L = mod_w.shape[0]
    tm_lat = 512 if T % 512 == 0 else 256
    tm_ctx = 256
    tq = 256

    n_rows = ((B + 1 + 7) // 8) * 8
    cond = jnp.concatenate([c, c_ctx[None, :], jnp.zeros((n_rows - B - 1, D), F32)], axis=0)
    mod = _modulation(cond, mod_w, mod_b).reshape(L, n_rows, 3, D)

    cos_t, sin_t = _rope_tables(T)
    hid = lax.broadcasted_iota(jnp.int32, (D_RWKV, D_RWKV), 0) // RWKV_HEAD
    ones_bd = (hid == hid.T).astype(BF16)
    lat_blocks = T // tm_lat
    ctx_blocks = Tc // tm_ctx

    x2 = x.reshape(B * T, D)
    xc2 = ctx.reshape(B * Tc, D)
    for l in range(L):
        need_ctx_out = l < L - 1
        w_in_bf = w_in[l].astype(BF16)
        w_out_bf = w_out[l].astype(BF16)
        pre_g = norm_pre_g[l][None, :]
        post_g = norm_post_g[l][None, :]
        lat_row = lambda i: i // lat_blocks
        ctx_row = lambda i: B
        pl_ = _inproj(x2, mod[l], lat_row, pre_g, w_in_bf, cos_t, sin_t,
                      tm=tm_lat, rope=True, seq_blocks=lat_blocks)
        pc_ = _inproj(xc2, mod[l], ctx_row, pre_g, w_in_bf, cos_t, sin_t,
                      tm=tm_ctx, rope=False, seq_blocks=1)
        prw_l, pcv_l, q_l, k_l, v_l, za_l = [a.reshape(B, T, -1) for a in pl_]
        prw_c, pcv_c, q_c, k_c, v_c, za_c = [a.reshape(B, Tc, -1) for a in pc_]

        scan_prm = (rwkv_w0[l], _pad_lora(rwkv_w_up[l]), rwkv_a0[l], _pad_lora(rwkv_a_up[l]),
                    rwkv_k_k[l][None, :], rwkv_k_a[l][None, :], ones_bd)
        yf_c, yb_c, s_ctx = _rwkv_scan(prw_c, scan_prm, None, with_output=need_ctx_out, bb=2)
        yf_l, yb_l, _ = _rwkv_scan(prw_l, scan_prm, s_ctx, with_output=True, bb=2)

        post_prm = (rwkv_a0[l], _pad_lora(rwkv_a_up[l]), rwkv_k_a[l][None, :],
                    rwkv_r_k[l].reshape(1, D_RWKV), rwkv_ln_g[l][None, :], rwkv_ln_b[l][None, :],
                    conv_w[l], ones_bd)
        y01_l = _mixpost(yf_l, yb_l, prw_l, pcv_l, post_prm)
        ya_l = _attention(q_l, za_l, [(k_l, v_l), (k_c, v_c)], diff_lambda[l], diff_subln_g[l][None, :],
                          l, tq=tq)
        x2 = _outproj(y01_l.reshape(B * T, -1), ya_l.reshape(B * T, -1), w_out_bf, x2, mod[l], lat_row,
                      post_g, tm=tm_lat)
        if need_ctx_out:
            y01_c = _mixpost(yf_c, yb_c, prw_c, pcv_c, post_prm)
            ya_c = _attention(q_c, za_c, [(k_c, v_c)], diff_lambda[l], diff_subln_g[l][None, :], l,
                              tq=min(tq, Tc))
            xc2 = _outproj(y01_c.reshape(B * Tc, -1), ya_c.reshape(B * Tc, -1), w_out_bf, xc2, mod[l],
                           ctx_row, post_g, tm=tm_ctx)
    return x2.reshape(B, T, D)
```

```python
import functools
import math

import jax
import jax.numpy as jnp
from jax import lax
from jax.experimental import pallas as pl
from jax.experimental.pallas import tpu as pltpu

F32 = jnp.float32
BF16 = jnp.bfloat16

D_MODEL = 1024
D_RWKV = 256
RWKV_HEAD = 64
LORA = 64
D_CONV = 256
CONV_WIDTH = 3
D_ATTN = 512
DIFF_HEAD_DIM = 64
DIFF_V_DIM = 128
DIFF_HEADS = 4
GRID_W = 64
ROPE_THETA = 10000.0
ROPE_AXIS_DIM = 32
NORM_EPS = 1e-6
RWKV_GN_EPS = 64e-5
D_IN = 4352
C_RWKV, C_CONV, C_Q, C_K, C_V, C_ZA = 0, 1280, 2304, 2816, 3328, 3840
N_RWKV = 1280
N_CONV = 1024

ATTN_ROWS = 128
CHUNK = 64
RWKV_BB = 4
PAIR = 128
VMEM_LIMIT = 56 * 1024 * 1024

NT_DIMS = (((1,), (1,)), ((), ()))
TN_DIMS = (((0,), (0,)), ((), ()))


def _cparams(sem):
    return pltpu.CompilerParams(dimension_semantics=sem, vmem_limit_bytes=VMEM_LIMIT)


def _silu(x):
    return x * jax.nn.sigmoid(x)


def _split3(x):
    hi = x.astype(BF16)
    r1 = x - hi.astype(F32)
    mid = r1.astype(BF16)
    lo = (r1 - mid.astype(F32)).astype(BF16)
    return hi, mid, lo


def _dot_exact_rhs(x, m):
    hi, mid, lo = _split3(x)
    return (jnp.dot(hi, m, preferred_element_type=F32)
            + jnp.dot(mid, m, preferred_element_type=F32)
            + jnp.dot(lo, m, preferred_element_type=F32))


def _mod_kernel(c_ref, w_ref, b_ref, o_ref):
    a = _silu(c_ref[...])
    o_ref[0] = jnp.dot(a, w_ref[0], preferred_element_type=F32,
                       precision=lax.Precision.HIGHEST) + b_ref[0]


def _modulation(cond, mod_w, mod_b):
    L = mod_w.shape[0]
    R = cond.shape[0]
    tn = 512
    return pl.pallas_call(
        _mod_kernel,
        grid=(L, 3 * D_MODEL // tn),
        in_specs=[pl.BlockSpec((R, D_MODEL), lambda l, j: (0, 0)),
                  pl.BlockSpec((1, D_MODEL, tn), lambda l, j: (l, 0, j)),
                  pl.BlockSpec((1, 1, tn), lambda l, j: (l, 0, j))],
        out_specs=pl.BlockSpec((1, R, tn), lambda l, j: (l, 0, j)),
        out_shape=jax.ShapeDtypeStruct((L, R, 3 * D_MODEL), F32),
        compiler_params=_cparams(("parallel", "parallel")),
        name="modulation",
    )(cond, mod_w, mod_b.reshape(L, 1, 3 * D_MODEL))


def _rope(t, cosv, sinv):
    lane = lax.broadcasted_iota(jnp.int32, t.shape, 1)
    first_half = (lane % 32) < 16
    partner = jnp.where(first_half, pltpu.roll(t, 112, 1), pltpu.roll(t, 16, 1))
    return t * cosv + partner * sinv


def _inproj_kernel(x_ref, mod_ref, g_ref, w_ref, cos_ref, sin_ref,
                   prw_ref, pcv_ref, q_ref, k_ref, v_ref, za_ref, *, rope):
    xf = x_ref[...]
    ms = jnp.mean(xf * xf, axis=-1, keepdims=True)
    y = xf * lax.rsqrt(ms + NORM_EPS) * g_ref[...]
    shift = mod_ref[0, 0:1, :]
    scale = mod_ref[0, 1:2, :]
    h = (y * (1.0 + scale) + shift).astype(BF16)

    def proj(c0, width):
        return jnp.dot(h, w_ref[:, c0:c0 + width], preferred_element_type=F32)

    for c in range(0, N_RWKV, 256):
        prw_ref[:, c:c + 256] = proj(C_RWKV + c, 256).astype(BF16)
    for c in range(0, N_CONV, 256):
        pcv_ref[:, c:c + 256] = proj(C_CONV + c, 256).astype(BF16)
    for c in range(0, D_ATTN, 256):
        v_ref[:, c:c + 256] = proj(C_V + c, 256).astype(BF16)
        za_ref[:, c:c + 256] = proj(C_ZA + c, 256).astype(BF16)
    qk_scale = DIFF_HEAD_DIM ** -0.5 * math.log2(math.e)
    if rope:
        cosv = cos_ref[...]
        sinv = sin_ref[...]
    for c in range(0, D_ATTN, 256):
        tq = proj(C_Q + c, 256)
        tk = proj(C_K + c, 256)
        for s in range(0, 256, 128):
            tqs = tq[:, s:s + 128]
            tks = tk[:, s:s + 128]
            if rope:
                tqs = _rope(tqs, cosv, sinv)
                tks = _rope(tks, cosv, sinv)
            q_ref[:, c + s:c + s + 128] = (tqs * qk_scale).astype(BF16)
            k_ref[:, c + s:c + s + 128] = tks.astype(BF16)


def _inproj(x2d, mod_l, mod_row_of_block, pre_g, w_in_bf, cos_t, sin_t, *, tm, rope, seq_blocks):
    M = x2d.shape[0]
    outs = [jax.ShapeDtypeStruct((M, n), BF16) for n in (N_RWKV, N_CONV, D_ATTN, D_ATTN, D_ATTN, D_ATTN)]
    row = lambda n: pl.BlockSpec((tm, n), lambda i: (i, 0))
    return pl.pallas_call(
        functools.partial(_inproj_kernel, rope=rope),
        grid=(M // tm,),
        in_specs=[row(D_MODEL),
                  pl.BlockSpec((1, 3, D_MODEL), lambda i: (mod_row_of_block(i), 0, 0)),
                  pl.BlockSpec((1, D_MODEL), lambda i: (0, 0)),
                  pl.BlockSpec((D_MODEL, D_IN), lambda i: (0, 0)),
                  pl.BlockSpec((tm, 128), lambda i: (i % seq_blocks, 0)),
                  pl.BlockSpec((tm, 128), lambda i: (i % seq_blocks, 0))],
        out_specs=[row(N_RWKV), row(N_CONV), row(D_ATTN), row(D_ATTN), row(D_ATTN), row(D_ATTN)],
        out_shape=outs,
        compiler_params=_cparams(("parallel",)),
        name="inproj_rope" if rope else "inproj",
    )(x2d, mod_l, pre_g, w_in_bf, cos_t, sin_t)


def _tri_mask(reverse, inclusive):
    t = lax.broadcasted_iota(jnp.int32, (PAIR, PAIR), 0) % CHUNK
    s = lax.broadcasted_iota(jnp.int32, (PAIR, PAIR), 1) % CHUNK
    if reverse:
        return (s >= t) if inclusive else (s > t)
    return (s <= t) if inclusive else (s < t)


def _expand_pair(x, p):
    xs = x[:, PAIR * p:PAIR * (p + 1)]
    lane = lax.broadcasted_iota(jnp.int32, xs.shape, 1)
    zero = jnp.zeros_like(xs)
    return jnp.concatenate([jnp.where(lane < RWKV_HEAD, xs, zero),
                            jnp.where(lane >= RWKV_HEAD, xs, zero)], axis=0)


def _dot(a, b):
    return jnp.dot(a, b, preferred_element_type=F32)


def _rwkv_step(Xs, dirs, prm, Hs, with_output):
    w0, wup, a0, aup, k_k, k_a, ones_bd = prm
    n = len(Xs)
    rng = range(n)
    prs = [(i, p) for i in rng for p in range(2)]
    r = [Xs[i][:, 0:256].astype(F32) for i in rng]
    k = [Xs[i][:, 256:512].astype(F32) for i in rng]
    v = [Xs[i][:, 512:768] for i in rng]
    tlw = [jnp.tanh(Xs[i][:, 768:896].astype(F32)).astype(BF16) for i in rng]
    wraw = [w0[dirs[i]:dirs[i] + 1, :] + _dot(tlw[i], wup[dirs[i]]) for i in rng]
    a = [jax.nn.sigmoid(a0[dirs[i]:dirs[i] + 1, :] + _dot(Xs[i][:, 896:1024], aup[dirs[i]])) for i in rng]
    kkr = [k[i] * k_k for i in rng]
    ss = [_dot_exact_rhs(kkr[i] * kkr[i], ones_bd) for i in rng]
    wlog = [-jnp.exp(-jax.nn.softplus(-wraw[i]) - 0.5) for i in rng]
    ti = lax.broadcasted_iota(jnp.int32, (CHUNK, CHUNK), 0)
    si = lax.broadcasted_iota(jnp.int32, (CHUNK, CHUNK), 1)
    ltri = [jnp.where(si <= ti, 1.0, 0.0).astype(BF16), jnp.where(si >= ti, 1.0, 0.0).astype(BF16)]
    wsp = [_split3(wlog[i]) for i in rng]
    g = [_dot(ltri[dirs[i]], wsp[i][0]) + _dot(ltri[dirs[i]], wsp[i][1]) + _dot(ltri[dirs[i]], wsp[i][2])
         for i in rng]
    kk = [kkr[i] * lax.rsqrt(jnp.maximum(ss[i], 1e-24)) for i in rng]
    kmod = [k[i] * (1.0 + (a[i] - 1.0) * k_a) for i in rng]
    bb = [kk[i] * a[i] for i in rng]
    g_end = [g[i][0:1, :] if dirs[i] else g[i][CHUNK - 1:CHUNK, :] for i in rng]
    eneg = [jnp.exp(-g[i]) for i in rng]
    ec = [jnp.exp(g_end[i] - g[i]) for i in rng]
    at = [(-kk[i] * jnp.exp(g[i] - wlog[i])).astype(BF16) for i in rng]
    bt = [(bb[i] * eneg[i]).astype(BF16) for i in rng]
    kt = [(kmod[i] * eneg[i]).astype(BF16) for i in rng]
    bh = [(bb[i] * ec[i]).astype(BF16) for i in rng]
    kh = [(kmod[i] * ec[i]).astype(BF16) for i in rng]
    decay_end = [jnp.exp(g_end[i]) for i in rng]

    strict = [_tri_mask(False, False), _tri_mask(True, False)]
    eye = (lax.broadcasted_iota(jnp.int32, (PAIR, PAIR), 0)
           == lax.broadcasted_iota(jnp.int32, (PAIR, PAIR), 1))
    Pa = [_expand_pair(at[i], p) for i, p in prs]
    Vx = [_expand_pair(v[i], p) for i, p in prs]
    rhs = [jnp.concatenate([_expand_pair(bt[i], p), _expand_pair(kt[i], p)], axis=0) for i, p in prs]
    BK = [jnp.concatenate([_expand_pair(bh[i], p), _expand_pair(kh[i], p)], axis=0) for i, p in prs]
    m = range(len(prs))
    if with_output:
        incl = [_tri_mask(False, True), _tri_mask(True, True)]
        rt = [(r[i] * jnp.exp(g[i])).astype(BF16) for i in rng]
        Pr = [_expand_pair(rt[i], p) for i, p in prs]
        A4 = [lax.dot_general(jnp.concatenate([Pa[j], Pr[j]], axis=0), rhs[j], NT_DIMS,
                              preferred_element_type=F32) for j in m]
        Ar = [jnp.where(jnp.concatenate([incl[dirs[prs[j][0]]]] * 2, axis=1), A4[j][PAIR:, :], 0.0)
              .astype(BF16) for j in m]
    else:
        A4 = [lax.dot_general(Pa[j], rhs[j], NT_DIMS, preferred_element_type=F32) for j in m]
    Aab = [jnp.where(strict[dirs[prs[j][0]]], A4[j][:PAIR, :PAIR], 0.0) for j in m]
    Aak = [jnp.where(strict[dirs[prs[j][0]]], A4[j][:PAIR, PAIR:], 0.0).astype(BF16) for j in m]
    Tm = [jnp.where(eye, 1.0, 0.0) + Aab[j] for j in m]
    Pw = [Aab[j].astype(BF16) for j in m]
    AkV = [_dot(Aak[j], Vx[j]).astype(BF16) for j in m]
    for _ in range(5):
        Pw = [_dot(Pw[j], Pw[j]).astype(BF16) for j in m]
        Tm = [Tm[j] + _dot(Tm[j].astype(BF16), Pw[j]) for j in m]
    WU = [_dot(Tm[j].astype(BF16), jnp.concatenate([Pa[j], AkV[j]], axis=1)).astype(BF16)
          for j in m]
    R2 = [jnp.concatenate([WU[j], jnp.concatenate([jnp.zeros_like(Vx[j]), Vx[j]], axis=1)], axis=0)
          for j in m]
    MH = [lax.dot_general(BK[j], R2[j], TN_DIMS, preferred_element_type=F32) for j in m]
    Hb = [Hs[j].astype(BF16) for j in m]
    gcol = [jnp.broadcast_to(decay_end[i][:, PAIR * p:PAIR * (p + 1)], (PAIR, PAIR)).T for i, p in prs]
    new_Hs = [gcol[j] * Hs[j] + _dot(MH[j][:, :PAIR].astype(BF16), Hb[j]) + MH[j][:, PAIR:] for j in m]
    ys = None
    if with_output:
        QY = [_dot(Ar[j], R2[j]) for j in m]
        Qh = [(Pr[j].astype(F32) + QY[j][:, :PAIR]).astype(BF16) for j in m]
        yx = [_dot(Qh[j], Hb[j]) + QY[j][:, PAIR:] for j in m]
        yp = [yx[j][:CHUNK] + yx[j][CHUNK:] for j in m]
        ys = [jnp.concatenate([yp[2 * i], yp[2 * i + 1]], axis=1) for i in rng]
    return ys, new_Hs


def _rwkv_kernel(*refs, bb, with_output, has_init):
    it = iter(refs)
    pf_ref = next(it)
    pb_ref = next(it)
    w0_ref, wup_ref, a0_ref, aup_ref, kk_ref, ka_ref, ones_ref = (next(it) for _ in range(7))
    s0_ref = next(it) if has_init else None
    if with_output:
        yf_ref = next(it)
        yb_ref = next(it)
    st_ref = next(it)
    j = pl.program_id(1)

    @pl.when(j == 0)
    def _():
        if has_init:
            st_ref[...] = s0_ref[...]
        else:
            st_ref[...] = jnp.zeros_like(st_ref)

    prm = (w0_ref[...], wup_ref[...], a0_ref[...], aup_ref[...], kk_ref[...], ka_ref[...], ones_ref[...])
    inst = [(b, d) for b in range(bb) for d in range(2)]
    Xs = [(pb_ref if d else pf_ref)[b] for b, d in inst]
    Hs = [st_ref[b, 2 * d + p] for b, d in inst for p in range(2)]
    ys, new_Hs = _rwkv_step(Xs, [d for _, d in inst], prm, Hs, with_output)
    for i, (b, d) in enumerate(inst):
        st_ref[b, 2 * d] = new_Hs[2 * i]
        st_ref[b, 2 * d + 1] = new_Hs[2 * i + 1]
        if with_output:
            (yb_ref if d else yf_ref)[b] = ys[i].astype(BF16)


def _rwkv_scan(p_rwkv, prm, init_state, *, with_output, bb):
    B, T, _ = p_rwkv.shape
    n = T // CHUNK
    has_init = init_state is not None
    full = lambda a: pl.BlockSpec(a.shape, lambda i, j: (0,) * a.ndim)
    tok = lambda width, rev: pl.BlockSpec(
        (bb, CHUNK, width), (lambda i, j: (i, n - 1 - j, 0)) if rev else (lambda i, j: (i, j, 0)))
    st_spec = pl.BlockSpec((bb, 4, PAIR, PAIR), lambda i, j: (i, 0, 0, 0))
    in_specs = [tok(N_RWKV, False), tok(N_RWKV, True)] + [full(a) for a in prm]
    args = [p_rwkv, p_rwkv] + list(prm)
    if has_init:
        in_specs.append(st_spec)
        args.append(init_state)
    out_specs = []
    out_shape = []
    if with_output:
        out_specs += [tok(D_RWKV, False), tok(D_RWKV, True)]
        out_shape += [jax.ShapeDtypeStruct((B, T, D_RWKV), BF16)] * 2
    out_specs.append(st_spec)
    out_shape.append(jax.ShapeDtypeStruct((B, 4, PAIR, PAIR), F32))
    res = pl.pallas_call(
        functools.partial(_rwkv_kernel, bb=bb, with_output=with_output, has_init=has_init),
        grid=(B // bb, n),
        in_specs=in_specs,
        out_specs=out_specs,
        out_shape=out_shape,
        compiler_params=_cparams(("parallel", "arbitrary")),
        name="rwkv_scan" if with_output else "rwkv_state",
    )(*args)
    if with_output:
        return res[0], res[1], res[2]
    return None, None, res[0]


def _mixpost_kernel(yf_ref, yb_ref, prw_ref, pcv_ref, a0_ref, aup_ref, ka_ref, rk_ref,
                    lng_ref, lnb_ref, cw_ref, ones_ref, o_ref, *, rows):
    T = o_ref.shape[1]
    ones_bd = ones_ref[...]
    mean_bd = ones_bd * (1.0 / RWKV_HEAD)
    for c0 in range(0, T, rows):
        sl = slice(c0, c0 + rows)
        y = yf_ref[0, sl, :].astype(F32) + yb_ref[0, sl, :].astype(F32)
        mu = _dot_exact_rhs(y, mean_bd)
        yc = y - mu
        var = _dot_exact_rhs(yc * yc, mean_bd)
        yn = yc * lax.rsqrt(var + RWKV_GN_EPS) * lng_ref[...] + lnb_ref[...]
        r = prw_ref[0, sl, 0:256].astype(F32)
        k = prw_ref[0, sl, 256:512].astype(F32)
        v = prw_ref[0, sl, 512:768].astype(F32)
        la = prw_ref[0, sl, 896:1024]
        z = prw_ref[0, sl, 1024:1280].astype(F32)
        a_f = jax.nn.sigmoid(a0_ref[0:1, :] + jnp.dot(la, aup_ref[0], preferred_element_type=F32))
        a_b = jax.nn.sigmoid(a0_ref[1:2, :] + jnp.dot(la, aup_ref[1], preferred_element_type=F32))
        ksum = k * (2.0 + (a_f + a_b - 2.0) * ka_ref[...])
        bonus = _dot_exact_rhs(r * ksum * rk_ref[...], ones_bd) * v
        o_ref[0, sl, 0:256] = ((yn + bonus) * _silu(z)).astype(BF16)
        lo = max(c0 - 8, 0)
        hi = min(c0 + rows + 8, T)
        u = pcv_ref[0, lo:hi, 256:512].astype(F32) * pcv_ref[0, lo:hi, 512:768].astype(F32)
        n_u = hi - lo
        off = c0 - lo
        trow = lax.broadcasted_iota(jnp.int32, (n_u, D_CONV), 0) + lo
        u_prev = jnp.where(trow == 0, 0.0, pltpu.roll(u, 1, 0))
        u_next = jnp.where(trow == T - 1, 0.0, pltpu.roll(u, n_u - 1, 0))
        conv = (u_prev * cw_ref[0:1, :] + u * cw_ref[1:2, :] + u_next * cw_ref[2:3, :])[off:off + rows]
        bg = pcv_ref[0, sl, 0:256].astype(F32)
        zc = pcv_ref[0, sl, 768:1024].astype(F32)
        o_ref[0, sl, 256:512] = (bg * conv * _silu(zc)).astype(BF16)


def _mixpost(yf, yb, p_rwkv, p_conv, prm):
    B, T, _ = p_rwkv.shape
    rows = min(T, 256)
    full = lambda a: pl.BlockSpec(a.shape, lambda b: (0,) * a.ndim)
    seq = lambda n: pl.BlockSpec((1, T, n), lambda b: (b, 0, 0))
    return pl.pallas_call(
        functools.partial(_mixpost_kernel, rows=rows),
        grid=(B,),
        in_specs=[seq(D_RWKV), seq(D_RWKV), seq(N_RWKV), seq(N_CONV)] + [full(a) for a in prm],
        out_specs=seq(D_RWKV + D_CONV),
        out_shape=jax.ShapeDtypeStruct((B, T, D_RWKV + D_CONV), BF16),
        compiler_params=_cparams(("parallel",)),
        name="mixpost",
    )(yf, yb, p_rwkv, p_conv, *prm)


def _attn_kernel(*refs, n_kv, lam_init, rows):
    dl_ref, g_ref, q_ref, z_ref = refs[:4]
    kv_refs = refs[4:4 + 2 * n_kv]
    o_ref = refs[4 + 2 * n_kv]
    tq = q_ref.shape[1]
    ks = [kv_refs[2 * i][0] for i in range(n_kv)]
    vs = [kv_refs[2 * i + 1][0] for i in range(n_kv)]
    add = lambda a, b: a + b
    groups = [(r0, sub) for r0 in range(0, tq, rows) for sub in range(2)]
    lane = lax.broadcasted_iota(jnp.int32, (rows, DIFF_V_DIM), 1)
    sel = [lane < DIFF_HEAD_DIM, lane >= DIFF_HEAD_DIM]
    qm = [jnp.where(sel[sub], q_ref[0, r0:r0 + rows, :], 0) for r0, sub in groups]
    n = range(len(groups))
    s = [[lax.dot_general(qm[g], kk, NT_DIMS, preferred_element_type=F32) for kk in ks] for g in n]
    m = [functools.reduce(jnp.maximum, [jnp.max(x, axis=-1, keepdims=True) for x in s[g]]) for g in n]
    p = [[jnp.exp2(x - m[g]) for x in s[g]] for g in n]
    l = [functools.reduce(add, [jnp.sum(x, axis=-1, keepdims=True) for x in p[g]]) for g in n]
    o = [functools.reduce(add, [_dot(x.astype(BF16), vv) for x, vv in zip(p[g], vs)]) for g in n]
    dl = dl_ref[...]
    lam = (jnp.exp(jnp.sum(dl[0:1] * dl[1:2], axis=-1, keepdims=True))
           - jnp.exp(jnp.sum(dl[2:3] * dl[3:4], axis=-1, keepdims=True)) + lam_init)
    for i, r0 in enumerate(range(0, tq, rows)):
        od = o[2 * i] * (1.0 / l[2 * i]) - o[2 * i + 1] * (lam / l[2 * i + 1])
        ms = jnp.mean(od * od, axis=-1, keepdims=True)
        od = od * lax.rsqrt(ms + NORM_EPS) * g_ref[...] * (1.0 - lam_init)
        o_ref[0, r0:r0 + rows, :] = (od * _silu(z_ref[0, r0:r0 + rows, :].astype(F32))).astype(BF16)


def _attention(q, z, kvs, diff_lambda, subln_g, layer_idx, *, tq):
    B, Tq, _ = q.shape
    lam_init = 0.8 - 0.6 * math.exp(-0.3 * layer_idx)
    qspec = pl.BlockSpec((1, tq, DIFF_V_DIM), lambda b, h, i: (b, i, h))
    in_specs = [pl.BlockSpec(diff_lambda.shape, lambda b, h, i: (0, 0)),
                pl.BlockSpec((1, DIFF_V_DIM), lambda b, h, i: (0, 0)),
                qspec, qspec]
    args = [diff_lambda, subln_g, q, z]
    for k, v in kvs:
        Tk = k.shape[1]
        kvspec = pl.BlockSpec((1, Tk, DIFF_V_DIM), lambda b, h, i: (b, 0, h))
        in_specs += [kvspec, kvspec]
        args += [k, v]
    return pl.pallas_call(
        functools.partial(_attn_kernel, n_kv=len(kvs), lam_init=lam_init, rows=min(tq, ATTN_ROWS)),
        grid=(B, DIFF_HEADS, Tq // tq),
        in_specs=in_specs,
        out_specs=qspec,
        out_shape=jax.ShapeDtypeStruct((B, Tq, D_ATTN), BF16),
        compiler_params=_cparams(("parallel", "parallel", "arbitrary")),
        name="diff_attn",
    )(*args)


def _outproj_kernel(y01_ref, ya_ref, w_ref, x_ref, mod_ref, g_ref, o_ref):
    half = D_RWKV + D_CONV
    acc = (jnp.dot(y01_ref[...], w_ref[0:half, :], preferred_element_type=F32)
           + jnp.dot(ya_ref[...], w_ref[half:, :], preferred_element_type=F32))
    ms = jnp.mean(acc * acc, axis=-1, keepdims=True)
    yn = acc * lax.rsqrt(ms + NORM_EPS) * g_ref[...]
    o_ref[...] = x_ref[...] + mod_ref[0, 2:3, :] * yn


def _outproj(y01, ya, w_out_bf, x2d, mod_l, mod_row_of_block, post_g, *, tm):
    M = x2d.shape[0]
    row = lambda n: pl.BlockSpec((tm, n), lambda i: (i, 0))
    return pl.pallas_call(
        _outproj_kernel,
        grid=(M // tm,),
        in_specs=[row(D_RWKV + D_CONV), row(D_ATTN),
                  pl.BlockSpec((D_MODEL, D_MODEL), lambda i: (0, 0)),
                  row(D_MODEL),
                  pl.BlockSpec((1, 3, D_MODEL), lambda i: (mod_row_of_block(i), 0, 0)),
                  pl.BlockSpec((1, D_MODEL), lambda i: (0, 0))],
        out_specs=row(D_MODEL),
        out_shape=jax.ShapeDtypeStruct((M, D_MODEL), F32),
        compiler_params=_cparams(("parallel",)),
        name="outproj",
    )(y01, ya, w_out_bf, x2d, mod_l, post_g)


def _rope_tables(T):
    rows = T // GRID_W
    row = jnp.repeat(jnp.arange(rows, dtype=F32), GRID_W)
    col = jnp.tile(jnp.arange(GRID_W, dtype=F32), rows)
    inv_freq = ROPE_THETA ** (-jnp.arange(0, ROPE_AXIS_DIM, 2, dtype=F32) / ROPE_AXIS_DIM)
    ang_r = row[:, None] * inv_freq
    ang_c = col[:, None] * inv_freq
    cr, sr, cc, sc = jnp.cos(ang_r), jnp.sin(ang_r), jnp.cos(ang_c), jnp.sin(ang_c)
    cos64 = jnp.concatenate([cr, cr, cc, cc], axis=-1)
    sin64 = jnp.concatenate([-sr, sr, -sc, sc], axis=-1)
    return jnp.tile(cos64, (1, 2)), jnp.tile(sin64, (1, 2))


def _pad_lora(w_up):
    z = jnp.zeros_like(w_up[0])
    return jnp.stack([jnp.concatenate([w_up[0], z], axis=0),
                      jnp.concatenate([z, w_up[1]], axis=0)]).astype(BF16)


def kernel(x, c, ctx, c_ctx, mod_w, mod_b, norm_pre_g, norm_post_g, w_in, w_out, rwkv_w0, rwkv_w_up,
           rwkv_a0, rwkv_a_up, rwkv_k_k, rwkv_k_a, rwkv_r_k, rwkv_ln_g, rwkv_ln_b, conv_w, diff_lambda,
           diff_subln_g):
    B, T, D = x.shape
    Tc = ctx.shape[1]
    L = mod_w.shape[0]
    tm_lat = 512 if T % 512 == 0 else 256
    tm_ctx = 256
    tq = 512 if T % 512 == 0 else 256

    n_rows = ((B + 1 + 7) // 8) * 8
    cond = jnp.concatenate([c, c_ctx[None, :], jnp.zeros((n_rows - B - 1, D), F32)], axis=0)
    mod = _modulation(cond, mod_w, mod_b).reshape(L, n_rows, 3, D)

    cos_t, sin_t = _rope_tables(T)
    hid = lax.broadcasted_iota(jnp.int32, (D_RWKV, D_RWKV), 0) // RWKV_HEAD
    ones_bd = (hid == hid.T).astype(BF16)
    lat_blocks = T // tm_lat
    ctx_blocks = Tc // tm_ctx

    x2 = x.reshape(B * T, D)
    xc2 = ctx.reshape(B * Tc, D)
    for l in range(L):
        need_ctx_out = l < L - 1
        w_in_bf = w_in[l].astype(BF16)
        w_out_bf = w_out[l].astype(BF16)
        pre_g = norm_pre_g[l][None, :]
        post_g = norm_post_g[l][None, :]
        lat_row = lambda i: i // lat_blocks
        ctx_row = lambda i: B
        pl_ = _inproj(x2, mod[l], lat_row, pre_g, w_in_bf, cos_t, sin_t,
                      tm=tm_lat, rope=True, seq_blocks=lat_blocks)
        pc_ = _inproj(xc2, mod[l], ctx_row, pre_g, w_in_bf, cos_t, sin_t,
                      tm=tm_ctx, rope=False, seq_blocks=1)
        prw_l, pcv_l, q_l, k_l, v_l, za_l = [a.reshape(B, T, -1) for a in pl_]
        prw_c, pcv_c, q_c, k_c, v_c, za_c = [a.reshape(B, Tc, -1) for a in pc_]

        scan_prm = (rwkv_w0[l], _pad_lora(rwkv_w_up[l]), rwkv_a0[l], _pad_lora(rwkv_a_up[l]),
                    rwkv_k_k[l][None, :], rwkv_k_a[l][None, :], ones_bd)
        yf_c, yb_c, s_ctx = _rwkv_scan(prw_c, scan_prm, None, with_output=need_ctx_out, bb=RWKV_BB)
        yf_l, yb_l, _ = _rwkv_scan(prw_l, scan_prm, s_ctx, with_output=True, bb=RWKV_BB)

        post_prm = (rwkv_a0[l], _pad_lora(rwkv_a_up[l]), rwkv_k_a[l][None, :],
                    rwkv_r_k[l].reshape(1, D_RWKV), rwkv_ln_g[l][None, :], rwkv_ln_b[l][None, :],
                    conv_w[l], ones_bd)
        y01_l = _mixpost(yf_l, yb_l, prw_l, pcv_l, post_prm)
        ya_l = _attention(q_l, za_l, [(k_l, v_l), (k_c, v_c)], diff_lambda[l], diff_subln_g[l][None, :],
                          l, tq=tq)
        x2 = _outproj(y01_l.reshape(B * T, -1), ya_l.reshape(B * T, -1), w_out_bf, x2, mod[l], lat_row,
                      post_g, tm=tm_lat)
        if need_ctx_out:
            y01_c = _mixpost(yf_c, yb_c, prw_c, pcv_c, post_prm)
            ya_c = _attention(q_c, za_c, [(k_c, v_c)], diff_lambda[l], diff_subln_g[l][None, :], l,
                              tq=min(tq, Tc))
            xc2 = _outproj(y01_c.reshape(B * Tc, -1), ya_c.reshape(B * Tc, -1), w_out_bf, xc2, mod[l],
                           ctx_row, post_g, tm=tm_ctx)
    return x2.reshape(B, T, D)
```

```python
import functools
import math

import jax
import jax.numpy as jnp
from jax import lax
from jax.experimental import pallas as pl
from jax.experimental.pallas import tpu as pltpu

F32 = jnp.float32
BF16 = jnp.bfloat16

D_MODEL = 1024
D_RWKV = 256
RWKV_HEAD = 64
LORA = 64
D_CONV = 256
CONV_WIDTH = 3
D_ATTN = 512
DIFF_HEAD_DIM = 64
DIFF_V_DIM = 128
DIFF_HEADS = 4
GRID_W = 64
ROPE_THETA = 10000.0
ROPE_AXIS_DIM = 32
NORM_EPS = 1e-6
RWKV_GN_EPS = 64e-5
D_IN = 4352
C_RWKV, C_CONV, C_Q, C_K, C_V, C_ZA = 0, 1280, 2304, 2816, 3328, 3840
N_RWKV = 1280
N_CONV = 1024

ATTN_ROWS = 128
CHUNK = 64
RWKV_BB = 4
PAIR = 128
VMEM_LIMIT = 56 * 1024 * 1024

NT_DIMS = (((1,), (1,)), ((), ()))
TN_DIMS = (((0,), (0,)), ((), ()))


def _cparams(sem):
    return pltpu.CompilerParams(dimension_semantics=sem, vmem_limit_bytes=VMEM_LIMIT)


def _silu(x):
    return x * jax.nn.sigmoid(x)


def _split2(x):
    hi = x.astype(BF16)
    lo = (x - hi.astype(F32)).astype(BF16)
    return hi, lo


def _dot_exact_rhs(x, m):
    hi, lo = _split2(x)
    return jnp.dot(hi, m, preferred_element_type=F32) + jnp.dot(lo, m, preferred_element_type=F32)


def _mod_kernel(c_ref, w_ref, b_ref, o_ref):
    a = _silu(c_ref[...])
    o_ref[0] = jnp.dot(a, w_ref[0], preferred_element_type=F32,
                       precision=lax.Precision.HIGHEST) + b_ref[0]


def _modulation(cond, mod_w, mod_b):
    L = mod_w.shape[0]
    R = cond.shape[0]
    tn = 512
    return pl.pallas_call(
        _mod_kernel,
        grid=(L, 3 * D_MODEL // tn),
        in_specs=[pl.BlockSpec((R, D_MODEL), lambda l, j: (0, 0)),
                  pl.BlockSpec((1, D_MODEL, tn), lambda l, j: (l, 0, j)),
                  pl.BlockSpec((1, 1, tn), lambda l, j: (l, 0, j))],
        out_specs=pl.BlockSpec((1, R, tn), lambda l, j: (l, 0, j)),
        out_shape=jax.ShapeDtypeStruct((L, R, 3 * D_MODEL), F32),
        compiler_params=_cparams(("parallel", "parallel")),
        name="modulation",
    )(cond, mod_w, mod_b.reshape(L, 1, 3 * D_MODEL))


def _rope(t, cosv, sinv):
    lane = lax.broadcasted_iota(jnp.int32, t.shape, 1)
    first_half = (lane % 32) < 16
    partner = jnp.where(first_half, pltpu.roll(t, 112, 1), pltpu.roll(t, 16, 1))
    return t * cosv + partner * sinv


def _inproj_kernel(x_ref, mod_ref, g_ref, w_ref, cos_ref, sin_ref,
                   prw_ref, pcv_ref, q_ref, k_ref, v_ref, za_ref, *, rope, groups):
    rg = x_ref.shape[0] // groups
    shift = mod_ref[0, 0:1, :]
    scale = mod_ref[0, 1:2, :]
    rs = [slice(g * rg, (g + 1) * rg) for g in range(groups)]
    hs = []
    for r in rs:
        xf = x_ref[r, :]
        ms = jnp.mean(xf * xf, axis=-1, keepdims=True)
        y = xf * lax.rsqrt(ms + NORM_EPS) * g_ref[...]
        hs.append((y * (1.0 + scale) + shift).astype(BF16))

    def proj(g, c0):
        return jnp.dot(hs[g], w_ref[:, c0:c0 + 256], preferred_element_type=F32)

    qk_scale = DIFF_HEAD_DIM ** -0.5 * math.log2(math.e)
    for g, r in enumerate(rs):
        for c in range(0, N_RWKV, 256):
            prw_ref[r, c:c + 256] = proj(g, C_RWKV + c).astype(BF16)
        for c in range(0, N_CONV, 256):
            pcv_ref[r, c:c + 256] = proj(g, C_CONV + c).astype(BF16)
        for c in range(0, D_ATTN, 256):
            v_ref[r, c:c + 256] = proj(g, C_V + c).astype(BF16)
            za_ref[r, c:c + 256] = proj(g, C_ZA + c).astype(BF16)
        if rope:
            cosv = cos_ref[r, :]
            sinv = sin_ref[r, :]
        for c in range(0, D_ATTN, 256):
            tq = proj(g, C_Q + c)
            tk = proj(g, C_K + c)
            for s in range(0, 256, 128):
                tqs = tq[:, s:s + 128]
                tks = tk[:, s:s + 128]
                if rope:
                    tqs = _rope(tqs, cosv, sinv)
                    tks = _rope(tks, cosv, sinv)
                q_ref[r, c + s:c + s + 128] = (tqs * qk_scale).astype(BF16)
                k_ref[r, c + s:c + s + 128] = tks.astype(BF16)


def _inproj(x2d, mod_l, mod_row_of_block, pre_g, w_in_bf, cos_t, sin_t, *, tm, rope, seq_blocks):
    M = x2d.shape[0]
    outs = [jax.ShapeDtypeStruct((M, n), BF16) for n in (N_RWKV, N_CONV, D_ATTN, D_ATTN, D_ATTN, D_ATTN)]
    row = lambda n: pl.BlockSpec((tm, n), lambda i: (i, 0))
    return pl.pallas_call(
        functools.partial(_inproj_kernel, rope=rope, groups=2 if tm >= 512 else 1),
        grid=(M // tm,),
        in_specs=[row(D_MODEL),
                  pl.BlockSpec((1, 3, D_MODEL), lambda i: (mod_row_of_block(i), 0, 0)),
                  pl.BlockSpec((1, D_MODEL), lambda i: (0, 0)),
                  pl.BlockSpec((D_MODEL, D_IN), lambda i: (0, 0), pipeline_mode=pl.Buffered(1)),
                  pl.BlockSpec((tm, 128), lambda i: (i % seq_blocks, 0)),
                  pl.BlockSpec((tm, 128), lambda i: (i % seq_blocks, 0))],
        out_specs=[row(N_RWKV), row(N_CONV), row(D_ATTN), row(D_ATTN), row(D_ATTN), row(D_ATTN)],
        out_shape=outs,
        compiler_params=_cparams(("parallel",)),
        name="inproj_rope" if rope else "inproj",
    )(x2d, mod_l, pre_g, w_in_bf, cos_t, sin_t)


def _tri_mask(reverse, inclusive):
    t = lax.broadcasted_iota(jnp.int32, (PAIR, PAIR), 0) % CHUNK
    s = lax.broadcasted_iota(jnp.int32, (PAIR, PAIR), 1) % CHUNK
    if reverse:
        return (s >= t) if inclusive else (s > t)
    return (s <= t) if inclusive else (s < t)


def _expand_pair(x, p):
    xs = x[:, PAIR * p:PAIR * (p + 1)]
    lane = lax.broadcasted_iota(jnp.int32, xs.shape, 1)
    zero = jnp.zeros_like(xs)
    return jnp.concatenate([jnp.where(lane < RWKV_HEAD, xs, zero),
                            jnp.where(lane >= RWKV_HEAD, xs, zero)], axis=0)


def _dot(a, b):
    return jnp.dot(a, b, preferred_element_type=F32)


def _rwkv_step(Xs, dirs, prm, Hs, with_output):
    w0, wup, a0, aup, k_k, k_a, ones_bd = prm
    n = len(Xs)
    rng = range(n)
    prs = [(i, p) for i in rng for p in range(2)]
    r = [Xs[i][:, 0:256].astype(F32) for i in rng]
    k = [Xs[i][:, 256:512].astype(F32) for i in rng]
    v = [Xs[i][:, 512:768] for i in rng]
    tlw = [jnp.tanh(Xs[i][:, 768:896].astype(F32)).astype(BF16) for i in rng]
    wraw = [w0[dirs[i]:dirs[i] + 1, :] + _dot(tlw[i], wup[dirs[i]]) for i in rng]
    a = [jax.nn.sigmoid(a0[dirs[i]:dirs[i] + 1, :] + _dot(Xs[i][:, 896:1024], aup[dirs[i]])) for i in rng]
    kkr = [k[i] * k_k for i in rng]
    ss = [_dot_exact_rhs(kkr[i] * kkr[i], ones_bd) for i in rng]
    wlog = [-jnp.exp(-jax.nn.softplus(-wraw[i]) - 0.5) for i in rng]
    ti = lax.broadcasted_iota(jnp.int32, (CHUNK, CHUNK), 0)
    si = lax.broadcasted_iota(jnp.int32, (CHUNK, CHUNK), 1)
    ltri = [jnp.where(si <= ti, 1.0, 0.0).astype(BF16), jnp.where(si >= ti, 1.0, 0.0).astype(BF16)]
    wsp = [_split2(wlog[i]) for i in rng]
    g = [_dot(ltri[dirs[i]], wsp[i][0]) + _dot(ltri[dirs[i]], wsp[i][1]) for i in rng]
    kk = [kkr[i] * lax.rsqrt(jnp.maximum(ss[i], 1e-24)) for i in rng]
    kmod = [k[i] * (1.0 + (a[i] - 1.0) * k_a) for i in rng]
    bb = [kk[i] * a[i] for i in rng]
    g_end = [g[i][0:1, :] if dirs[i] else g[i][CHUNK - 1:CHUNK, :] for i in rng]
    eneg = [jnp.exp(-g[i]) for i in rng]
    ec = [jnp.exp(g_end[i] - g[i]) for i in rng]
    at = [(-kk[i] * jnp.exp(g[i] - wlog[i])).astype(BF16) for i in rng]
    bt = [(bb[i] * eneg[i]).astype(BF16) for i in rng]
    kt = [(kmod[i] * eneg[i]).astype(BF16) for i in rng]
    bh = [(bb[i] * ec[i]).astype(BF16) for i in rng]
    kh = [(kmod[i] * ec[i]).astype(BF16) for i in rng]
    decay_end = [jnp.exp(g_end[i]) for i in rng]

    strict = [_tri_mask(False, False), _tri_mask(True, False)]
    eye = (lax.broadcasted_iota(jnp.int32, (PAIR, PAIR), 0)
           == lax.broadcasted_iota(jnp.int32, (PAIR, PAIR), 1))
    Pa = [_expand_pair(at[i], p) for i, p in prs]
    Vx = [_expand_pair(v[i], p) for i, p in prs]
    rhs = [jnp.concatenate([_expand_pair(bt[i], p), _expand_pair(kt[i], p)], axis=0) for i, p in prs]
    BK = [jnp.concatenate([_expand_pair(bh[i], p), _expand_pair(kh[i], p)], axis=0) for i, p in prs]
    m = range(len(prs))
    if with_output:
        incl = [_tri_mask(False, True), _tri_mask(True, True)]
        rt = [(r[i] * jnp.exp(g[i])).astype(BF16) for i in rng]
        Pr = [_expand_pair(rt[i], p) for i, p in prs]
        A4 = [lax.dot_general(jnp.concatenate([Pa[j], Pr[j]], axis=0), rhs[j], NT_DIMS,
                              preferred_element_type=F32) for j in m]
        Ar = [jnp.where(jnp.concatenate([incl[dirs[prs[j][0]]]] * 2, axis=1), A4[j][PAIR:, :], 0.0)
              .astype(BF16) for j in m]
    else:
        A4 = [lax.dot_general(Pa[j], rhs[j], NT_DIMS, preferred_element_type=F32) for j in m]
    Aab = [jnp.where(strict[dirs[prs[j][0]]], A4[j][:PAIR, :PAIR], 0.0) for j in m]
    Aak = [jnp.where(strict[dirs[prs[j][0]]], A4[j][:PAIR, PAIR:], 0.0).astype(BF16) for j in m]
    Tm = [jnp.where(eye, 1.0, 0.0) + Aab[j] for j in m]
    Pw = [Aab[j].astype(BF16) for j in m]
    AkV = [_dot(Aak[j], Vx[j]).astype(BF16) for j in m]
    for _ in range(5):
        Pw = [_dot(Pw[j], Pw[j]).astype(BF16) for j in m]
        Tm = [Tm[j] + _dot(Tm[j].astype(BF16), Pw[j]) for j in m]
    WU = [_dot(Tm[j].astype(BF16), jnp.concatenate([Pa[j], AkV[j]], axis=1)).astype(BF16)
          for j in m]
    R2 = [jnp.concatenate([WU[j], jnp.concatenate([jnp.zeros_like(Vx[j]), Vx[j]], axis=1)], axis=0)
          for j in m]
    MH = [lax.dot_general(BK[j], R2[j], TN_DIMS, preferred_element_type=F32) for j in m]
    Hb = [Hs[j].astype(BF16) for j in m]
    gcol = [jnp.broadcast_to(decay_end[i][:, PAIR * p:PAIR * (p + 1)], (PAIR, PAIR)).T for i, p in prs]
    new_Hs = [gcol[j] * Hs[j] + _dot(MH[j][:, :PAIR].astype(BF16), Hb[j]) + MH[j][:, PAIR:] for j in m]
    ys = None
    if with_output:
        QY = [_dot(Ar[j], R2[j]) for j in m]
        Qh = [(Pr[j].astype(F32) + QY[j][:, :PAIR]).astype(BF16) for j in m]
        yx = [_dot(Qh[j], Hb[j]) + QY[j][:, PAIR:] for j in m]
        yp = [yx[j][:CHUNK] + yx[j][CHUNK:] for j in m]
        ys = [jnp.concatenate([yp[2 * i], yp[2 * i + 1]], axis=1) for i in rng]
    return ys, new_Hs


def _rwkv_kernel(*refs, bb, with_output, has_init):
    it = iter(refs)
    pf_ref = next(it)
    pb_ref = next(it)
    w0_ref, wup_ref, a0_ref, aup_ref, kk_ref, ka_ref, ones_ref = (next(it) for _ in range(7))
    s0_ref = next(it) if has_init else None
    if with_output:
        yf_ref = next(it)
        yb_ref = next(it)
    st_ref = next(it)
    j = pl.program_id(1)

    @pl.when(j == 0)
    def _():
        if has_init:
            st_ref[...] = s0_ref[...]
        else:
            st_ref[...] = jnp.zeros_like(st_ref)

    prm = (w0_ref[...], wup_ref[...], a0_ref[...], aup_ref[...], kk_ref[...], ka_ref[...], ones_ref[...])
    inst = [(b, d) for b in range(bb) for d in range(2)]
    Xs = [(pb_ref if d else pf_ref)[b] for b, d in inst]
    Hs = [st_ref[b, 2 * d + p] for b, d in inst for p in range(2)]
    ys, new_Hs = _rwkv_step(Xs, [d for _, d in inst], prm, Hs, with_output)
    for i, (b, d) in enumerate(inst):
        st_ref[b, 2 * d] = new_Hs[2 * i]
        st_ref[b, 2 * d + 1] = new_Hs[2 * i + 1]
        if with_output:
            (yb_ref if d else yf_ref)[b] = ys[i].astype(BF16)


def _rwkv_scan(p_rwkv, prm, init_state, *, with_output, bb):
    B, T, _ = p_rwkv.shape
    n = T // CHUNK
    has_init = init_state is not None
    full = lambda a: pl.BlockSpec(a.shape, lambda i, j: (0,) * a.ndim)
    tok = lambda width, rev: pl.BlockSpec(
        (bb, CHUNK, width), (lambda i, j: (i, n - 1 - j, 0)) if rev else (lambda i, j: (i, j, 0)))
    st_spec = pl.BlockSpec((bb, 4, PAIR, PAIR), lambda i, j: (i, 0, 0, 0))
    in_specs = [tok(N_RWKV, False), tok(N_RWKV, True)] + [full(a) for a in prm]
    args = [p_rwkv, p_rwkv] + list(prm)
    if has_init:
        in_specs.append(st_spec)
        args.append(init_state)
    out_specs = []
    out_shape = []
    if with_output:
        out_specs += [tok(D_RWKV, False), tok(D_RWKV, True)]
        out_shape += [jax.ShapeDtypeStruct((B, T, D_RWKV), BF16)] * 2
    out_specs.append(st_spec)
    out_shape.append(jax.ShapeDtypeStruct((B, 4, PAIR, PAIR), F32))
    res = pl.pallas_call(
        functools.partial(_rwkv_kernel, bb=bb, with_output=with_output, has_init=has_init),
        grid=(B // bb, n),
        in_specs=in_specs,
        out_specs=out_specs,
        out_shape=out_shape,
        compiler_params=_cparams(("parallel", "arbitrary")),
        name="rwkv_scan" if with_output else "rwkv_state",
    )(*args)
    if with_output:
        return res[0], res[1], res[2]
    return None, None, res[0]


def _mixpost_kernel(yf_ref, yb_ref, prw_ref, pcv_ref, a0_ref, aup_ref, ka_ref, rk_ref,
                    lng_ref, lnb_ref, cw_ref, ones_ref, o_ref, *, rows):
    T = o_ref.shape[1]
    ones_bd = ones_ref[...]
    mean_bd = ones_bd * (1.0 / RWKV_HEAD)
    for c0 in range(0, T, rows):
        sl = slice(c0, c0 + rows)
        y = yf_ref[0, sl, :].astype(F32) + yb_ref[0, sl, :].astype(F32)
        mu = _dot_exact_rhs(y, mean_bd)
        yc = y - mu
        var = _dot_exact_rhs(yc * yc, mean_bd)
        yn = yc * lax.rsqrt(var + RWKV_GN_EPS) * lng_ref[...] + lnb_ref[...]
        r = prw_ref[0, sl, 0:256].astype(F32)
        k = prw_ref[0, sl, 256:512].astype(F32)
        v = prw_ref[0, sl, 512:768].astype(F32)
        la = prw_ref[0, sl, 896:1024]
        z = prw_ref[0, sl, 1024:1280].astype(F32)
        a_f = jax.nn.sigmoid(a0_ref[0:1, :] + jnp.dot(la, aup_ref[0], preferred_element_type=F32))
        a_b = jax.nn.sigmoid(a0_ref[1:2, :] + jnp.dot(la, aup_ref[1], preferred_element_type=F32))
        ksum = k * (2.0 + (a_f + a_b - 2.0) * ka_ref[...])
        bonus = _dot_exact_rhs(r * ksum * rk_ref[...], ones_bd) * v
        o_ref[0, sl, 0:256] = ((yn + bonus) * _silu(z)).astype(BF16)
        lo = max(c0 - 8, 0)
        hi = min(c0 + rows + 8, T)
        u = pcv_ref[0, lo:hi, 256:512].astype(F32) * pcv_ref[0, lo:hi, 512:768].astype(F32)
        n_u = hi - lo
        off = c0 - lo
        trow = lax.broadcasted_iota(jnp.int32, (n_u, D_CONV), 0) + lo
        u_prev = jnp.where(trow == 0, 0.0, pltpu.roll(u, 1, 0))
        u_next = jnp.where(trow == T - 1, 0.0, pltpu.roll(u, n_u - 1, 0))
        conv = (u_prev * cw_ref[0:1, :] + u * cw_ref[1:2, :] + u_next * cw_ref[2:3, :])[off:off + rows]
        bg = pcv_ref[0, sl, 0:256].astype(F32)
        zc = pcv_ref[0, sl, 768:1024].astype(F32)
        o_ref[0, sl, 256:512] = (bg * conv * _silu(zc)).astype(BF16)


def _mixpost(yf, yb, p_rwkv, p_conv, prm):
    B, T, _ = p_rwkv.shape
    rows = min(T, 256)
    full = lambda a: pl.BlockSpec(a.shape, lambda b: (0,) * a.ndim)
    seq = lambda n: pl.BlockSpec((1, T, n), lambda b: (b, 0, 0))
    return pl.pallas_call(
        functools.partial(_mixpost_kernel, rows=rows),
        grid=(B,),
        in_specs=[seq(D_RWKV), seq(D_RWKV), seq(N_RWKV), seq(N_CONV)] + [full(a) for a in prm],
        out_specs=seq(D_RWKV + D_CONV),
        out_shape=jax.ShapeDtypeStruct((B, T, D_RWKV + D_CONV), BF16),
        compiler_params=_cparams(("parallel",)),
        name="mixpost",
    )(yf, yb, p_rwkv, p_conv, *prm)


def _attn_kernel(*refs, n_kv, lam_init, rows):
    dl_ref, g_ref, q_ref, z_ref = refs[:4]
    kv_refs = refs[4:4 + 2 * n_kv]
    o_ref = refs[4 + 2 * n_kv]
    tq = q_ref.shape[1]
    ks = [kv_refs[2 * i][0] for i in range(n_kv)]
    vs = [kv_refs[2 * i + 1][0] for i in range(n_kv)]
    add = lambda a, b: a + b
    groups = [(r0, sub) for r0 in range(0, tq, rows) for sub in range(2)]
    lane = lax.broadcasted_iota(jnp.int32, (rows, DIFF_V_DIM), 1)
    sel = [lane < DIFF_HEAD_DIM, lane >= DIFF_HEAD_DIM]
    qm = [jnp.where(sel[sub], q_ref[0, r0:r0 + rows, :], 0) for r0, sub in groups]
    n = range(len(groups))
    s = [[lax.dot_general(qm[g], kk, NT_DIMS, preferred_element_type=F32) for kk in ks] for g in n]
    m = [functools.reduce(jnp.maximum, [jnp.max(x, axis=-1, keepdims=True) for x in s[g]]) for g in n]
    p = [[jnp.exp2(x - m[g]) for x in s[g]] for g in n]
    l = [functools.reduce(add, [jnp.sum(x, axis=-1, keepdims=True) for x in p[g]]) for g in n]
    o = [functools.reduce(add, [_dot(x.astype(BF16), vv) for x, vv in zip(p[g], vs)]) for g in n]
    dl = dl_ref[...]
    lam = (jnp.exp(jnp.sum(dl[0:1] * dl[1:2], axis=-1, keepdims=True))
           - jnp.exp(jnp.sum(dl[2:3] * dl[3:4], axis=-1, keepdims=True)) + lam_init)
    for i, r0 in enumerate(range(0, tq, rows)):
        od = o[2 * i] * (1.0 / l[2 * i]) - o[2 * i + 1] * (lam / l[2 * i + 1])
        ms = jnp.mean(od * od, axis=-1, keepdims=True)
        od = od * lax.rsqrt(ms + NORM_EPS) * g_ref[...] * (1.0 - lam_init)
        o_ref[0, r0:r0 + rows, :] = (od * _silu(z_ref[0, r0:r0 + rows, :].astype(F32))).astype(BF16)


def _attention(q, z, kvs, diff_lambda, subln_g, layer_idx, *, tq):
    B, Tq, _ = q.shape
    lam_init = 0.8 - 0.6 * math.exp(-0.3 * layer_idx)
    qspec = pl.BlockSpec((1, tq, DIFF_V_DIM), lambda b, h, i: (b, i, h))
    in_specs = [pl.BlockSpec(diff_lambda.shape, lambda b, h, i: (0, 0)),
                pl.BlockSpec((1, DIFF_V_DIM), lambda b, h, i: (0, 0)),
                qspec, qspec]
    args = [diff_lambda, subln_g, q, z]
    for k, v in kvs:
        Tk = k.shape[1]
        kvspec = pl.BlockSpec((1, Tk, DIFF_V_DIM), lambda b, h, i: (b, 0, h))
        in_specs += [kvspec, kvspec]
        args += [k, v]
    return pl.pallas_call(
        functools.partial(_attn_kernel, n_kv=len(kvs), lam_init=lam_init, rows=min(tq, ATTN_ROWS)),
        grid=(B, DIFF_HEADS, Tq // tq),
        in_specs=in_specs,
        out_specs=qspec,
        out_shape=jax.ShapeDtypeStruct((B, Tq, D_ATTN), BF16),
        compiler_params=_cparams(("parallel", "parallel", "arbitrary")),
        name="diff_attn",
    )(*args)


def _outproj_kernel(y01_ref, ya_ref, w_ref, x_ref, mod_ref, g_ref, o_ref, *, groups):
    half = D_RWKV + D_CONV
    rg = x_ref.shape[0] // groups
    rs = [slice(g * rg, (g + 1) * rg) for g in range(groups)]
    accs = [_dot(y01_ref[r, :], w_ref[0:half, :]) + _dot(ya_ref[r, :], w_ref[half:, :]) for r in rs]
    for r, acc in zip(rs, accs):
        ms = jnp.mean(acc * acc, axis=-1, keepdims=True)
        yn = acc * lax.rsqrt(ms + NORM_EPS) * g_ref[...]
        o_ref[r, :] = x_ref[r, :] + mod_ref[0, 2:3, :] * yn


def _outproj(y01, ya, w_out_bf, x2d, mod_l, mod_row_of_block, post_g, *, tm):
    M = x2d.shape[0]
    row = lambda n: pl.BlockSpec((tm, n), lambda i: (i, 0))
    return pl.pallas_call(
        functools.partial(_outproj_kernel, groups=2 if tm >= 512 else 1),
        grid=(M // tm,),
        in_specs=[row(D_RWKV + D_CONV), row(D_ATTN),
                  pl.BlockSpec((D_MODEL, D_MODEL), lambda i: (0, 0)),
                  row(D_MODEL),
                  pl.BlockSpec((1, 3, D_MODEL), lambda i: (mod_row_of_block(i), 0, 0)),
                  pl.BlockSpec((1, D_MODEL), lambda i: (0, 0))],
        out_specs=row(D_MODEL),
        out_shape=jax.ShapeDtypeStruct((M, D_MODEL), F32),
        compiler_params=_cparams(("parallel",)),
        name="outproj",
    )(y01, ya, w_out_bf, x2d, mod_l, post_g)


def _rope_tables(T):
    rows = T // GRID_W
    row = jnp.repeat(jnp.arange(rows, dtype=F32), GRID_W)
    col = jnp.tile(jnp.arange(GRID_W, dtype=F32), rows)
    inv_freq = ROPE_THETA ** (-jnp.arange(0, ROPE_AXIS_DIM, 2, dtype=F32) / ROPE_AXIS_DIM)
    ang_r = row[:, None] * inv_freq
    ang_c = col[:, None] * inv_freq
    cr, sr, cc, sc = jnp.cos(ang_r), jnp.sin(ang_r), jnp.cos(ang_c), jnp.sin(ang_c)
    cos64 = jnp.concatenate([cr, cr, cc, cc], axis=-1)
    sin64 = jnp.concatenate([-sr, sr, -sc, sc], axis=-1)
    return jnp.tile(cos64, (1, 2)), jnp.tile(sin64, (1, 2))


def _pad_lora(w_up):
    z = jnp.zeros_like(w_up[0])
    return jnp.stack([jnp.concatenate([w_up[0], z], axis=0),
                      jnp.concatenate([z, w_up[1]], axis=0)]).astype(BF16)


def kernel(x, c, ctx, c_ctx, mod_w, mod_b, norm_pre_g, norm_post_g, w_in, w_out, rwkv_w0, rwkv_w_up,
           rwkv_a0, rwkv_a_up, rwkv_k_k, rwkv_k_a, rwkv_r_k, rwkv_ln_g, rwkv_ln_b, conv_w, diff_lambda,
           diff_subln_g):
    B, T, D = x.shape
    Tc = ctx.shape[1]
    L = mod_w.shape[0]
    tm_lat = 512 if T % 512 == 0 else 256
    tm_ctx = 256
    tq = 512 if T % 512 == 0 else 256

    n_rows = ((B + 1 + 7) // 8) * 8
    cond = jnp.concatenate([c, c_ctx[None, :], jnp.zeros((n_rows - B - 1, D), F32)], axis=0)
    mod = _modulation(cond, mod_w, mod_b).reshape(L, n_rows, 3, D)

    cos_t, sin_t = _rope_tables(T)
    hid = lax.broadcasted_iota(jnp.int32, (D_RWKV, D_RWKV), 0) // RWKV_HEAD
    ones_bd = (hid == hid.T).astype(BF16)
    lat_blocks = T // tm_lat
    ctx_blocks = Tc // tm_ctx

    x2 = x.reshape(B * T, D)
    xc2 = ctx.reshape(B * Tc, D)
    for l in range(L):
        need_ctx_out = l < L - 1
        w_in_bf = w_in[l].astype(BF16)
        w_out_bf = w_out[l].astype(BF16)
        pre_g = norm_pre_g[l][None, :]
        post_g = norm_post_g[l][None, :]
        lat_row = lambda i: i // lat_blocks
        ctx_row = lambda i: B
        pl_ = _inproj(x2, mod[l], lat_row, pre_g, w_in_bf, cos_t, sin_t,
                      tm=tm_lat, rope=True, seq_blocks=lat_blocks)
        pc_ = _inproj(xc2, mod[l], ctx_row, pre_g, w_in_bf, cos_t, sin_t,
                      tm=tm_ctx, rope=False, seq_blocks=1)
        prw_l, pcv_l, q_l, k_l, v_l, za_l = [a.reshape(B, T, -1) for a in pl_]
        prw_c, pcv_c, q_c, k_c, v_c, za_c = [a.reshape(B, Tc, -1) for a in pc_]

        scan_prm = (rwkv_w0[l], _pad_lora(rwkv_w_up[l]), rwkv_a0[l], _pad_lora(rwkv_a_up[l]),
                    rwkv_k_k[l][None, :], rwkv_k_a[l][None, :], ones_bd)
        yf_c, yb_c, s_ctx = _rwkv_scan(prw_c, scan_prm, None, with_output=need_ctx_out, bb=RWKV_BB)
        yf_l, yb_l, _ = _rwkv_scan(prw_l, scan_prm, s_ctx, with_output=True, bb=RWKV_BB)

        post_prm = (rwkv_a0[l], _pad_lora(rwkv_a_up[l]), rwkv_k_a[l][None, :],
                    rwkv_r_k[l].reshape(1, D_RWKV), rwkv_ln_g[l][None, :], rwkv_ln_b[l][None, :],
                    conv_w[l], ones_bd)
        y01_l = _mixpost(yf_l, yb_l, prw_l, pcv_l, post_prm)
        ya_l = _attention(q_l, za_l, [(k_l, v_l), (k_c, v_c)], diff_lambda[l], diff_subln_g[l][None, :],
                          l, tq=tq)
        x2 = _outproj(y01_l.reshape(B * T, -1), ya_l.reshape(B * T, -1), w_out_bf, x2, mod[l], lat_row,
                      post_g, tm=tm_lat)
        if need_ctx_out:
            y01_c = _mixpost(yf_c, yb_c, prw_c, pcv_c, post_prm)
            ya_c = _attention(q_c, za_c, [(k_c, v_c)], diff_lambda[l], diff_subln_g[l][None, :], l,
                              tq=min(tq, Tc))
            xc2 = _outproj(y01_c.reshape(B * Tc, -1), ya_c.reshape(B * Tc, -1), w_out_bf, xc2, mod[l],
                           ctx_row, post_g, tm=tm_ctx)
    return x2.reshape(B, T, D)
```

```python
import functools
import math

import jax
import jax.numpy as jnp
from jax import lax
from jax.experimental import pallas as pl
from jax.experimental.pallas import tpu as pltpu

F32 = jnp.float32
BF16 = jnp.bfloat16

D_MODEL = 1024
D_RWKV = 256
RWKV_HEAD = 64
LORA = 64
D_CONV = 256
CONV_WIDTH = 3
D_ATTN = 512
DIFF_HEAD_DIM = 64
DIFF_V_DIM = 128
DIFF_HEADS = 4
GRID_W = 64
ROPE_THETA = 10000.0
ROPE_AXIS_DIM = 32
NORM_EPS = 1e-6
RWKV_GN_EPS = 64e-5
D_IN = 4352
C_RWKV, C_CONV, C_Q, C_K, C_V, C_ZA = 0, 1280, 2304, 2816, 3328, 3840
N_RWKV = 1280
N_CONV = 1024

ATTN_ROWS = 128
CHUNK = 64
RWKV_BB = 4
PAIR = 128
VMEM_LIMIT = 56 * 1024 * 1024

NT_DIMS = (((1,), (1,)), ((), ()))
TN_DIMS = (((0,), (0,)), ((), ()))


def _cparams(sem):
    return pltpu.CompilerParams(dimension_semantics=sem, vmem_limit_bytes=VMEM_LIMIT)


def _silu(x):
    return x * jax.nn.sigmoid(x)


def _split2(x):
    hi = x.astype(BF16)
    lo = (x - hi.astype(F32)).astype(BF16)
    return hi, lo


def _dot_exact_rhs(x, m):
    hi, lo = _split2(x)
    return jnp.dot(hi, m, preferred_element_type=F32) + jnp.dot(lo, m, preferred_element_type=F32)


def _mod_kernel(c_ref, w_ref, b_ref, o_ref):
    a = _silu(c_ref[...])
    o_ref[0] = jnp.dot(a, w_ref[0], preferred_element_type=F32,
                       precision=lax.Precision.HIGHEST) + b_ref[0]


def _modulation(cond, mod_w, mod_b):
    L = mod_w.shape[0]
    R = cond.shape[0]
    tn = 512
    return pl.pallas_call(
        _mod_kernel,
        grid=(L, 3 * D_MODEL // tn),
        in_specs=[pl.BlockSpec((R, D_MODEL), lambda l, j: (0, 0)),
                  pl.BlockSpec((1, D_MODEL, tn), lambda l, j: (l, 0, j)),
                  pl.BlockSpec((1, 1, tn), lambda l, j: (l, 0, j))],
        out_specs=pl.BlockSpec((1, R, tn), lambda l, j: (l, 0, j)),
        out_shape=jax.ShapeDtypeStruct((L, R, 3 * D_MODEL), F32),
        compiler_params=_cparams(("parallel", "parallel")),
        name="modulation",
    )(cond, mod_w, mod_b.reshape(L, 1, 3 * D_MODEL))


def _rope(t, cosv, sinv):
    lane = lax.broadcasted_iota(jnp.int32, t.shape, 1)
    first_half = (lane % 32) < 16
    partner = jnp.where(first_half, pltpu.roll(t, 112, 1), pltpu.roll(t, 16, 1))
    return t * cosv + partner * sinv


def _inproj_kernel(x_ref, mod_ref, g_ref, w_ref, cos_ref, sin_ref,
                   prw_ref, pcv_ref, q_ref, k_ref, v_ref, za_ref, *, rope, groups):
    rg = x_ref.shape[0] // groups
    shift = mod_ref[0, 0:1, :]
    scale = mod_ref[0, 1:2, :]
    rs = [slice(g * rg, (g + 1) * rg) for g in range(groups)]
    hs = []
    for r in rs:
        xf = x_ref[r, :]
        ms = jnp.mean(xf * xf, axis=-1, keepdims=True)
        y = xf * lax.rsqrt(ms + NORM_EPS) * g_ref[...]
        hs.append((y * (1.0 + scale) + shift).astype(BF16))

    def proj(g, c0):
        return jnp.dot(hs[g], w_ref[:, c0:c0 + 256], preferred_element_type=F32)

    qk_scale = DIFF_HEAD_DIM ** -0.5 * math.log2(math.e)
    for g, r in enumerate(rs):
        for c in range(0, N_RWKV, 256):
            prw_ref[r, c:c + 256] = proj(g, C_RWKV + c).astype(BF16)
        for c in range(0, N_CONV, 256):
            pcv_ref[r, c:c + 256] = proj(g, C_CONV + c).astype(BF16)
        for c in range(0, D_ATTN, 256):
            v_ref[r, c:c + 256] = proj(g, C_V + c).astype(BF16)
            za_ref[r, c:c + 256] = proj(g, C_ZA + c).astype(BF16)
        if rope:
            cosv = cos_ref[r, :]
            sinv = sin_ref[r, :]
        for c in range(0, D_ATTN, 256):
            tq = proj(g, C_Q + c)
            tk = proj(g, C_K + c)
            for s in range(0, 256, 128):
                tqs = tq[:, s:s + 128]
                tks = tk[:, s:s + 128]
                if rope:
                    tqs = _rope(tqs, cosv, sinv)
                    tks = _rope(tks, cosv, sinv)
                q_ref[r, c + s:c + s + 128] = (tqs * qk_scale).astype(BF16)
                k_ref[r, c + s:c + s + 128] = tks.astype(BF16)


def _inproj(x2d, mod_l, mod_row_of_block, pre_g, w_in_bf, cos_t, sin_t, *, tm, rope, seq_blocks):
    M = x2d.shape[0]
    outs = [jax.ShapeDtypeStruct((M, n), BF16) for n in (N_RWKV, N_CONV, D_ATTN, D_ATTN, D_ATTN, D_ATTN)]
    row = lambda n: pl.BlockSpec((tm, n), lambda i: (i, 0))
    return pl.pallas_call(
        functools.partial(_inproj_kernel, rope=rope, groups=2 if tm >= 512 else 1),
        grid=(M // tm,),
        in_specs=[row(D_MODEL),
                  pl.BlockSpec((1, 3, D_MODEL), lambda i: (mod_row_of_block(i), 0, 0)),
                  pl.BlockSpec((1, D_MODEL), lambda i: (0, 0)),
                  pl.BlockSpec((D_MODEL, D_IN), lambda i: (0, 0), pipeline_mode=pl.Buffered(1)),
                  pl.BlockSpec((tm, 128), lambda i: (i % seq_blocks, 0)),
                  pl.BlockSpec((tm, 128), lambda i: (i % seq_blocks, 0))],
        out_specs=[row(N_RWKV), row(N_CONV), row(D_ATTN), row(D_ATTN), row(D_ATTN), row(D_ATTN)],
        out_shape=outs,
        compiler_params=_cparams(("parallel",)),
        name="inproj_rope" if rope else "inproj",
    )(x2d, mod_l, pre_g, w_in_bf, cos_t, sin_t)


def _tri_mask(reverse, inclusive):
    t = lax.broadcasted_iota(jnp.int32, (PAIR, PAIR), 0) % CHUNK
    s = lax.broadcasted_iota(jnp.int32, (PAIR, PAIR), 1) % CHUNK
    if reverse:
        return (s >= t) if inclusive else (s > t)
    return (s <= t) if inclusive else (s < t)


def _expand_pair(x, p):
    xs = x[:, PAIR * p:PAIR * (p + 1)]
    lane = lax.broadcasted_iota(jnp.int32, xs.shape, 1)
    zero = jnp.zeros_like(xs)
    return jnp.concatenate([jnp.where(lane < RWKV_HEAD, xs, zero),
                            jnp.where(lane >= RWKV_HEAD, xs, zero)], axis=0)


def _dot(a, b):
    return jnp.dot(a, b, preferred_element_type=F32)


def _rwkv_step(Xs, dirs, prm, Hs, with_output):
    w0, wup, a0, aup, k_k, k_a, ones_bd = prm
    n = len(Xs)
    rng = range(n)
    prs = [(i, p) for i in rng for p in range(2)]
    r = [Xs[i][:, 0:256].astype(F32) for i in rng]
    k = [Xs[i][:, 256:512].astype(F32) for i in rng]
    v = [Xs[i][:, 512:768] for i in rng]
    tlw = [jnp.tanh(Xs[i][:, 768:896].astype(F32)).astype(BF16) for i in rng]
    wraw = [w0[dirs[i]:dirs[i] + 1, :] + _dot(tlw[i], wup[dirs[i]]) for i in rng]
    a = [jax.nn.sigmoid(a0[dirs[i]:dirs[i] + 1, :] + _dot(Xs[i][:, 896:1024], aup[dirs[i]])) for i in rng]
    kkr = [k[i] * k_k for i in rng]
    ss = [_dot_exact_rhs(kkr[i] * kkr[i], ones_bd) for i in rng]
    wlog = [-jnp.exp(-jax.nn.softplus(-wraw[i]) - 0.5) for i in rng]
    ti = lax.broadcasted_iota(jnp.int32, (CHUNK, CHUNK), 0)
    si = lax.broadcasted_iota(jnp.int32, (CHUNK, CHUNK), 1)
    ltri = [jnp.where(si <= ti, 1.0, 0.0).astype(BF16), jnp.where(si >= ti, 1.0, 0.0).astype(BF16)]
    wsp = [_split2(wlog[i]) for i in rng]
    g = [_dot(ltri[dirs[i]], wsp[i][0]) + _dot(ltri[dirs[i]], wsp[i][1]) for i in rng]
    kk = [kkr[i] * lax.rsqrt(jnp.maximum(ss[i], 1e-24)) for i in rng]
    kmod = [k[i] * (1.0 + (a[i] - 1.0) * k_a) for i in rng]
    bb = [kk[i] * a[i] for i in rng]
    g_end = [g[i][0:1, :] if dirs[i] else g[i][CHUNK - 1:CHUNK, :] for i in rng]
    eneg = [jnp.exp(-g[i]) for i in rng]
    ec = [jnp.exp(g_end[i] - g[i]) for i in rng]
    at = [(-kk[i] * jnp.exp(g[i] - wlog[i])).astype(BF16) for i in rng]
    bt = [(bb[i] * eneg[i]).astype(BF16) for i in rng]
    kt = [(kmod[i] * eneg[i]).astype(BF16) for i in rng]
    bh = [(bb[i] * ec[i]).astype(BF16) for i in rng]
    kh = [(kmod[i] * ec[i]).astype(BF16) for i in rng]
    decay_end = [jnp.exp(g_end[i]) for i in rng]

    strict = [_tri_mask(False, False), _tri_mask(True, False)]
    eye = (lax.broadcasted_iota(jnp.int32, (PAIR, PAIR), 0)
           == lax.broadcasted_iota(jnp.int32, (PAIR, PAIR), 1))
    Pa = [_expand_pair(at[i], p) for i, p in prs]
    Vx = [_expand_pair(v[i], p) for i, p in prs]
    rhs = [jnp.concatenate([_expand_pair(bt[i], p), _expand_pair(kt[i], p)], axis=0) for i, p in prs]
    BK = [jnp.concatenate([_expand_pair(bh[i], p), _expand_pair(kh[i], p)], axis=0) for i, p in prs]
    m = range(len(prs))
    if with_output:
        incl = [_tri_mask(False, True), _tri_mask(True, True)]
        rt = [(r[i] * jnp.exp(g[i])).astype(BF16) for i in rng]
        Pr = [_expand_pair(rt[i], p) for i, p in prs]
        A4 = [lax.dot_general(jnp.concatenate([Pa[j], Pr[j]], axis=0), rhs[j], NT_DIMS,
                              preferred_element_type=F32) for j in m]
        Ar = [jnp.where(jnp.concatenate([incl[dirs[prs[j][0]]]] * 2, axis=1), A4[j][PAIR:, :], 0.0)
              .astype(BF16) for j in m]
    else:
        A4 = [lax.dot_general(Pa[j], rhs[j], NT_DIMS, preferred_element_type=F32) for j in m]
    Aab = [jnp.where(strict[dirs[prs[j][0]]], A4[j][:PAIR, :PAIR], 0.0) for j in m]
    Aak = [jnp.where(strict[dirs[prs[j][0]]], A4[j][:PAIR, PAIR:], 0.0).astype(BF16) for j in m]
    Tm = [jnp.where(eye, 1.0, 0.0) + Aab[j] for j in m]
    Pw = [Aab[j].astype(BF16) for j in m]
    AkV = [_dot(Aak[j], Vx[j]).astype(BF16) for j in m]
    for _ in range(5):
        Pw = [_dot(Pw[j], Pw[j]).astype(BF16) for j in m]
        Tm = [Tm[j] + _dot(Tm[j].astype(BF16), Pw[j]) for j in m]
    WU = [_dot(Tm[j].astype(BF16), jnp.concatenate([Pa[j], AkV[j]], axis=1)).astype(BF16)
          for j in m]
    R2 = [jnp.concatenate([WU[j], jnp.concatenate([jnp.zeros_like(Vx[j]), Vx[j]], axis=1)], axis=0)
          for j in m]
    MH = [lax.dot_general(BK[j], R2[j], TN_DIMS, preferred_element_type=F32) for j in m]
    Hb = [Hs[j].astype(BF16) for j in m]
    gcol = [jnp.broadcast_to(decay_end[i][:, PAIR * p:PAIR * (p + 1)], (PAIR, PAIR)).T for i, p in prs]
    new_Hs = [gcol[j] * Hs[j] + _dot(MH[j][:, :PAIR].astype(BF16), Hb[j]) + MH[j][:, PAIR:] for j in m]
    ys = None
    if with_output:
        QY = [_dot(Ar[j], R2[j]) for j in m]
        Qh = [(Pr[j].astype(F32) + QY[j][:, :PAIR]).astype(BF16) for j in m]
        yx = [_dot(Qh[j], Hb[j]) + QY[j][:, PAIR:] for j in m]
        yp = [yx[j][:CHUNK] + yx[j][CHUNK:] for j in m]
        ys = [jnp.concatenate([yp[2 * i], yp[2 * i + 1]], axis=1) for i in rng]
    return ys, new_Hs


def _rwkv_kernel(*refs, bb, with_output, has_init):
    it = iter(refs)
    pf_ref = next(it)
    pb_ref = next(it)
    w0_ref, wup_ref, a0_ref, aup_ref, kk_ref, ka_ref, ones_ref = (next(it) for _ in range(7))
    s0_ref = next(it) if has_init else None
    if with_output:
        yf_ref = next(it)
        yb_ref = next(it)
    st_ref = next(it)
    j = pl.program_id(1)

    @pl.when(j == 0)
    def _():
        if has_init:
            st_ref[...] = s0_ref[...]
        else:
            st_ref[...] = jnp.zeros_like(st_ref)

    prm = (w0_ref[...], wup_ref[...], a0_ref[...], aup_ref[...], kk_ref[...], ka_ref[...], ones_ref[...])
    inst = [(b, d) for b in range(bb) for d in range(2)]
    Xs = [(pb_ref if d else pf_ref)[b] for b, d in inst]
    Hs = [st_ref[b, 2 * d + p] for b, d in inst for p in range(2)]
    ys, new_Hs = _rwkv_step(Xs, [d for _, d in inst], prm, Hs, with_output)
    for i, (b, d) in enumerate(inst):
        st_ref[b, 2 * d] = new_Hs[2 * i]
        st_ref[b, 2 * d + 1] = new_Hs[2 * i + 1]
        if with_output:
            (yb_ref if d else yf_ref)[b] = ys[i].astype(BF16)


def _rwkv_scan(p_rwkv, prm, init_state, *, with_output, bb):
    B, T, _ = p_rwkv.shape
    n = T // CHUNK
    has_init = init_state is not None
    full = lambda a: pl.BlockSpec(a.shape, lambda i, j: (0,) * a.ndim)
    tok = lambda width, rev: pl.BlockSpec(
        (bb, CHUNK, width), (lambda i, j: (i, n - 1 - j, 0)) if rev else (lambda i, j: (i, j, 0)))
    st_spec = pl.BlockSpec((bb, 4, PAIR, PAIR), lambda i, j: (i, 0, 0, 0))
    in_specs = [tok(N_RWKV, False), tok(N_RWKV, True)] + [full(a) for a in prm]
    args = [p_rwkv, p_rwkv] + list(prm)
    if has_init:
        in_specs.append(st_spec)
        args.append(init_state)
    out_specs = []
    out_shape = []
    if with_output:
        out_specs += [tok(D_RWKV, False), tok(D_RWKV, True)]
        out_shape += [jax.ShapeDtypeStruct((B, T, D_RWKV), BF16)] * 2
    out_specs.append(st_spec)
    out_shape.append(jax.ShapeDtypeStruct((B, 4, PAIR, PAIR), F32))
    res = pl.pallas_call(
        functools.partial(_rwkv_kernel, bb=bb, with_output=with_output, has_init=has_init),
        grid=(B // bb, n),
        in_specs=in_specs,
        out_specs=out_specs,
        out_shape=out_shape,
        compiler_params=_cparams(("parallel", "arbitrary")),
        name="rwkv_scan" if with_output else "rwkv_state",
    )(*args)
    if with_output:
        return res[0], res[1], res[2]
    return None, None, res[0]


def _mixpost_kernel(yf_ref, yb_ref, prw_ref, pcv_ref, a0_ref, aup_ref, ka_ref, rk_ref,
                    lng_ref, lnb_ref, cw_ref, ones_ref, o_ref, *, rows):
    T = o_ref.shape[1]
    ones_bd = ones_ref[...]
    mean_bd = ones_bd * (1.0 / RWKV_HEAD)
    for c0 in range(0, T, rows):
        sl = slice(c0, c0 + rows)
        y = yf_ref[0, sl, :].astype(F32) + yb_ref[0, sl, :].astype(F32)
        mu = _dot_exact_rhs(y, mean_bd)
        yc = y - mu
        var = _dot_exact_rhs(yc * yc, mean_bd)
        yn = yc * lax.rsqrt(var + RWKV_GN_EPS) * lng_ref[...] + lnb_ref[...]
        r = prw_ref[0, sl, 0:256].astype(F32)
        k = prw_ref[0, sl, 256:512].astype(F32)
        v = prw_ref[0, sl, 512:768].astype(F32)
        la = prw_ref[0, sl, 896:1024]
        z = prw_ref[0, sl, 1024:1280].astype(F32)
        a_f = jax.nn.sigmoid(a0_ref[0:1, :] + jnp.dot(la, aup_ref[0], preferred_element_type=F32))
        a_b = jax.nn.sigmoid(a0_ref[1:2, :] + jnp.dot(la, aup_ref[1], preferred_element_type=F32))
        ksum = k * (2.0 + (a_f + a_b - 2.0) * ka_ref[...])
        bonus = _dot_exact_rhs(r * ksum * rk_ref[...], ones_bd) * v
        o_ref[0, sl, 0:256] = ((yn + bonus) * _silu(z)).astype(BF16)
        lo = max(c0 - 8, 0)
        hi = min(c0 + rows + 8, T)
        u = pcv_ref[0, lo:hi, 256:512].astype(F32) * pcv_ref[0, lo:hi, 512:768].astype(F32)
        n_u = hi - lo
        off = c0 - lo
        trow = lax.broadcasted_iota(jnp.int32, (n_u, D_CONV), 0) + lo
        u_prev = jnp.where(trow == 0, 0.0, pltpu.roll(u, 1, 0))
        u_next = jnp.where(trow == T - 1, 0.0, pltpu.roll(u, n_u - 1, 0))
        conv = (u_prev * cw_ref[0:1, :] + u * cw_ref[1:2, :] + u_next * cw_ref[2:3, :])[off:off + rows]
        bg = pcv_ref[0, sl, 0:256].astype(F32)
        zc = pcv_ref[0, sl, 768:1024].astype(F32)
        o_ref[0, sl, 256:512] = (bg * conv * _silu(zc)).astype(BF16)


def _mixpost(yf, yb, p_rwkv, p_conv, prm):
    B, T, _ = p_rwkv.shape
    rows = min(T, 256)
    full = lambda a: pl.BlockSpec(a.shape, lambda b: (0,) * a.ndim)
    seq = lambda n: pl.BlockSpec((1, T, n), lambda b: (b, 0, 0))
    return pl.pallas_call(
        functools.partial(_mixpost_kernel, rows=rows),
        grid=(B,),
        in_specs=[seq(D_RWKV), seq(D_RWKV), seq(N_RWKV), seq(N_CONV)] + [full(a) for a in prm],
        out_specs=seq(D_RWKV + D_CONV),
        out_shape=jax.ShapeDtypeStruct((B, T, D_RWKV + D_CONV), BF16),
        compiler_params=_cparams(("parallel",)),
        name="mixpost",
    )(yf, yb, p_rwkv, p_conv, *prm)


def _mixout_kernel(yf_ref, yb_ref, prw_ref, pcv_ref, pprev_ref, pnext_ref, ya_ref, w_ref, x_ref, mod_ref,
                   g_ref, a0_ref, aup_ref, ka_ref, rk_ref, lng_ref, lnb_ref, cw_ref, ones_ref, o_ref,
                   *, rows, seq_blocks):
    tm = x_ref.shape[0]
    seq_len = seq_blocks * tm
    t0 = (pl.program_id(0) % seq_blocks) * tm
    ones_bd = ones_ref[...]
    mean_bd = ones_bd * (1.0 / RWKV_HEAD)
    u_of = lambda ref: ref[:, 256:512].astype(F32) * ref[:, 512:768].astype(F32)
    u = jnp.concatenate([u_of(pprev_ref), u_of(pcv_ref), u_of(pnext_ref)], axis=0)
    n_u = tm + 16
    trow = lax.broadcasted_iota(jnp.int32, (n_u, D_CONV), 0) + (t0 - 8)
    u_prev = jnp.where(trow == 0, 0.0, pltpu.roll(u, 1, 0))
    u_next = jnp.where(trow == seq_len - 1, 0.0, pltpu.roll(u, n_u - 1, 0))
    conv = (u_prev * cw_ref[0:1, :] + u * cw_ref[1:2, :] + u_next * cw_ref[2:3, :])[8:8 + tm]
    rs = [slice(r0, r0 + rows) for r0 in range(0, tm, rows)]
    y_rwkv, y_conv = [], []
    for sl in rs:
        y = yf_ref[sl, :].astype(F32) + yb_ref[sl, :].astype(F32)
        mu = _dot_exact_rhs(y, mean_bd)
        yc = y - mu
        var = _dot_exact_rhs(yc * yc, mean_bd)
        yn = yc * lax.rsqrt(var + RWKV_GN_EPS) * lng_ref[...] + lnb_ref[...]
        r = prw_ref[sl, 0:256].astype(F32)
        k = prw_ref[sl, 256:512].astype(F32)
        v = prw_ref[sl, 512:768].astype(F32)
        la = prw_ref[sl, 896:1024]
        z = prw_ref[sl, 1024:1280].astype(F32)
        a_f = jax.nn.sigmoid(a0_ref[0:1, :] + _dot(la, aup_ref[0]))
        a_b = jax.nn.sigmoid(a0_ref[1:2, :] + _dot(la, aup_ref[1]))
        ksum = k * (2.0 + (a_f + a_b - 2.0) * ka_ref[...])
        bonus = _dot_exact_rhs(r * ksum * rk_ref[...], ones_bd) * v
        y_rwkv.append(((yn + bonus) * _silu(z)).astype(BF16))
        bg = pcv_ref[sl, 0:256].astype(F32)
        zc = pcv_ref[sl, 768:1024].astype(F32)
        y_conv.append((bg * conv[sl] * _silu(zc)).astype(BF16))
    accs = [_dot(y_rwkv[i], w_ref[0:D_RWKV, :]) + _dot(y_conv[i], w_ref[D_RWKV:D_RWKV + D_CONV, :])
            + _dot(ya_ref[sl, :], w_ref[D_RWKV + D_CONV:, :]) for i, sl in enumerate(rs)]
    for sl, acc in zip(rs, accs):
        ms = jnp.mean(acc * acc, axis=-1, keepdims=True)
        yn = acc * lax.rsqrt(ms + NORM_EPS) * g_ref[...]
        o_ref[sl, :] = x_ref[sl, :] + mod_ref[0, 2:3, :] * yn


def _mixout(yf, yb, p_rwkv, p_conv, ya, w_out_bf, x2d, mod_l, mod_row_of_block, post_g, prm, *, tm, seq_blocks):
    M = x2d.shape[0]
    hb = tm // 8
    row = lambda n: pl.BlockSpec((tm, n), lambda i: (i, 0))
    full = lambda a: pl.BlockSpec(a.shape, lambda i: (0,) * a.ndim)
    return pl.pallas_call(
        functools.partial(_mixout_kernel, rows=min(tm, 256), seq_blocks=seq_blocks),
        grid=(M // tm,),
        in_specs=[row(D_RWKV), row(D_RWKV), row(N_RWKV), row(N_CONV),
                  pl.BlockSpec((8, N_CONV), lambda i: (jnp.maximum(i * hb - 1, 0), 0)),
                  pl.BlockSpec((8, N_CONV), lambda i: (jnp.minimum((i + 1) * hb, M // 8 - 1), 0)),
                  row(D_ATTN),
                  pl.BlockSpec((D_MODEL, D_MODEL), lambda i: (0, 0), pipeline_mode=pl.Buffered(1)),
                  row(D_MODEL),
                  pl.BlockSpec((1, 3, D_MODEL), lambda i: (mod_row_of_block(i), 0, 0)),
                  pl.BlockSpec((1, D_MODEL), lambda i: (0, 0))] + [full(a) for a in prm],
        out_specs=row(D_MODEL),
        out_shape=jax.ShapeDtypeStruct((M, D_MODEL), F32),
        compiler_params=_cparams(("parallel",)),
        name="mixout",
    )(yf, yb, p_rwkv, p_conv, p_conv, p_conv, ya, w_out_bf, x2d, mod_l, post_g, *prm)


def _attn_kernel(*refs, n_kv, lam_init, rows):
    dl_ref, g_ref, q_ref, z_ref = refs[:4]
    kv_refs = refs[4:4 + 2 * n_kv]
    o_ref = refs[4 + 2 * n_kv]
    tq = q_ref.shape[1]
    ks = [kv_refs[2 * i][0] for i in range(n_kv)]
    vs = [kv_refs[2 * i + 1][0] for i in range(n_kv)]
    add = lambda a, b: a + b
    groups = [(r0, sub) for r0 in range(0, tq, rows) for sub in range(2)]
    lane = lax.broadcasted_iota(jnp.int32, (rows, DIFF_V_DIM), 1)
    sel = [lane < DIFF_HEAD_DIM, lane >= DIFF_HEAD_DIM]
    qm = [jnp.where(sel[sub], q_ref[0, r0:r0 + rows, :], 0) for r0, sub in groups]
    n = range(len(groups))
    s = [[lax.dot_general(qm[g], kk, NT_DIMS, preferred_element_type=F32) for kk in ks] for g in n]
    m = [functools.reduce(jnp.maximum, [jnp.max(x, axis=-1, keepdims=True) for x in s[g]]) for g in n]
    p = [[jnp.exp2(x - m[g]) for x in s[g]] for g in n]
    l = [functools.reduce(add, [jnp.sum(x, axis=-1, keepdims=True) for x in p[g]]) for g in n]
    o = [functools.reduce(add, [_dot(x.astype(BF16), vv) for x, vv in zip(p[g], vs)]) for g in n]
    dl = dl_ref[...]
    lam = (jnp.exp(jnp.sum(dl[0:1] * dl[1:2], axis=-1, keepdims=True))
           - jnp.exp(jnp.sum(dl[2:3] * dl[3:4], axis=-1, keepdims=True)) + lam_init)
    for i, r0 in enumerate(range(0, tq, rows)):
        od = o[2 * i] * (1.0 / l[2 * i]) - o[2 * i + 1] * (lam / l[2 * i + 1])
        ms = jnp.mean(od * od, axis=-1, keepdims=True)
        od = od * lax.rsqrt(ms + NORM_EPS) * g_ref[...] * (1.0 - lam_init)
        o_ref[0, r0:r0 + rows, :] = (od * _silu(z_ref[0, r0:r0 + rows, :].astype(F32))).astype(BF16)


def _attention(q, z, kvs, diff_lambda, subln_g, layer_idx, *, tq):
    B, Tq, _ = q.shape
    lam_init = 0.8 - 0.6 * math.exp(-0.3 * layer_idx)
    qspec = pl.BlockSpec((1, tq, DIFF_V_DIM), lambda b, h, i: (b, i, h))
    in_specs = [pl.BlockSpec(diff_lambda.shape, lambda b, h, i: (0, 0)),
                pl.BlockSpec((1, DIFF_V_DIM), lambda b, h, i: (0, 0)),
                qspec, qspec]
    args = [diff_lambda, subln_g, q, z]
    for k, v in kvs:
        Tk = k.shape[1]
        kvspec = pl.BlockSpec((1, Tk, DIFF_V_DIM), lambda b, h, i: (b, 0, h))
        in_specs += [kvspec, kvspec]
        args += [k, v]
    return pl.pallas_call(
        functools.partial(_attn_kernel, n_kv=len(kvs), lam_init=lam_init, rows=min(tq, ATTN_ROWS)),
        grid=(B, DIFF_HEADS, Tq // tq),
        in_specs=in_specs,
        out_specs=qspec,
        out_shape=jax.ShapeDtypeStruct((B, Tq, D_ATTN), BF16),
        compiler_params=_cparams(("parallel", "parallel", "arbitrary")),
        name="diff_attn",
    )(*args)


def _outproj_kernel(y01_ref, ya_ref, w_ref, x_ref, mod_ref, g_ref, o_ref, *, groups):
    half = D_RWKV + D_CONV
    rg = x_ref.shape[0] // groups
    rs = [slice(g * rg, (g + 1) * rg) for g in range(groups)]
    accs = [_dot(y01_ref[r, :], w_ref[0:half, :]) + _dot(ya_ref[r, :], w_ref[half:, :]) for r in rs]
    for r, acc in zip(rs, accs):
        ms = jnp.mean(acc * acc, axis=-1, keepdims=True)
        yn = acc * lax.rsqrt(ms + NORM_EPS) * g_ref[...]
        o_ref[r, :] = x_ref[r, :] + mod_ref[0, 2:3, :] * yn


def _outproj(y01, ya, w_out_bf, x2d, mod_l, mod_row_of_block, post_g, *, tm):
    M = x2d.shape[0]
    row = lambda n: pl.BlockSpec((tm, n), lambda i: (i, 0))
    return pl.pallas_call(
        functools.partial(_outproj_kernel, groups=2 if tm >= 512 else 1),
        grid=(M // tm,),
        in_specs=[row(D_RWKV + D_CONV), row(D_ATTN),
                  pl.BlockSpec((D_MODEL, D_MODEL), lambda i: (0, 0)),
                  row(D_MODEL),
                  pl.BlockSpec((1, 3, D_MODEL), lambda i: (mod_row_of_block(i), 0, 0)),
                  pl.BlockSpec((1, D_MODEL), lambda i: (0, 0))],
        out_specs=row(D_MODEL),
        out_shape=jax.ShapeDtypeStruct((M, D_MODEL), F32),
        compiler_params=_cparams(("parallel",)),
        name="outproj",
    )(y01, ya, w_out_bf, x2d, mod_l, post_g)


def _rope_tables(T):
    rows = T // GRID_W
    row = jnp.repeat(jnp.arange(rows, dtype=F32), GRID_W)
    col = jnp.tile(jnp.arange(GRID_W, dtype=F32), rows)
    inv_freq = ROPE_THETA ** (-jnp.arange(0, ROPE_AXIS_DIM, 2, dtype=F32) / ROPE_AXIS_DIM)
    ang_r = row[:, None] * inv_freq
    ang_c = col[:, None] * inv_freq
    cr, sr, cc, sc = jnp.cos(ang_r), jnp.sin(ang_r), jnp.cos(ang_c), jnp.sin(ang_c)
    cos64 = jnp.concatenate([cr, cr, cc, cc], axis=-1)
    sin64 = jnp.concatenate([-sr, sr, -sc, sc], axis=-1)
    return jnp.tile(cos64, (1, 2)), jnp.tile(sin64, (1, 2))


def _pad_lora(w_up):
    z = jnp.zeros_like(w_up[0])
    return jnp.stack([jnp.concatenate([w_up[0], z], axis=0),
                      jnp.concatenate([z, w_up[1]], axis=0)]).astype(BF16)


def kernel(x, c, ctx, c_ctx, mod_w, mod_b, norm_pre_g, norm_post_g, w_in, w_out, rwkv_w0, rwkv_w_up,
           rwkv_a0, rwkv_a_up, rwkv_k_k, rwkv_k_a, rwkv_r_k, rwkv_ln_g, rwkv_ln_b, conv_w, diff_lambda,
           diff_subln_g):
    B, T, D = x.shape
    Tc = ctx.shape[1]
    L = mod_w.shape[0]
    tm_lat = 512 if T % 512 == 0 else 256
    tm_ctx = 256
    tq = 512 if T % 512 == 0 else 256

    n_rows = ((B + 1 + 7) // 8) * 8
    cond = jnp.concatenate([c, c_ctx[None, :], jnp.zeros((n_rows - B - 1, D), F32)], axis=0)
    mod = _modulation(cond, mod_w, mod_b).reshape(L, n_rows, 3, D)

    cos_t, sin_t = _rope_tables(T)
    hid = lax.broadcasted_iota(jnp.int32, (D_RWKV, D_RWKV), 0) // RWKV_HEAD
    ones_bd = (hid == hid.T).astype(BF16)
    lat_blocks = T // tm_lat
    ctx_blocks = Tc // tm_ctx

    x2 = x.reshape(B * T, D)
    xc2 = ctx.reshape(B * Tc, D)
    for l in range(L):
        need_ctx_out = l < L - 1
        w_in_bf = w_in[l].astype(BF16)
        w_out_bf = w_out[l].astype(BF16)
        pre_g = norm_pre_g[l][None, :]
        post_g = norm_post_g[l][None, :]
        lat_row = lambda i: i // lat_blocks
        ctx_row = lambda i: B
        pl_ = _inproj(x2, mod[l], lat_row, pre_g, w_in_bf, cos_t, sin_t,
                      tm=tm_lat, rope=True, seq_blocks=lat_blocks)
        pc_ = _inproj(xc2, mod[l], ctx_row, pre_g, w_in_bf, cos_t, sin_t,
                      tm=tm_ctx, rope=False, seq_blocks=1)
        prw_l, pcv_l, q_l, k_l, v_l, za_l = [a.reshape(B, T, -1) for a in pl_]
        prw_c, pcv_c, q_c, k_c, v_c, za_c = [a.reshape(B, Tc, -1) for a in pc_]

        scan_prm = (rwkv_w0[l], _pad_lora(rwkv_w_up[l]), rwkv_a0[l], _pad_lora(rwkv_a_up[l]),
                    rwkv_k_k[l][None, :], rwkv_k_a[l][None, :], ones_bd)
        yf_c, yb_c, s_ctx = _rwkv_scan(prw_c, scan_prm, None, with_output=need_ctx_out, bb=RWKV_BB)
        yf_l, yb_l, _ = _rwkv_scan(prw_l, scan_prm, s_ctx, with_output=True, bb=RWKV_BB)

        post_prm = (rwkv_a0[l], _pad_lora(rwkv_a_up[l]), rwkv_k_a[l][None, :],
                    rwkv_r_k[l].reshape(1, D_RWKV), rwkv_ln_g[l][None, :], rwkv_ln_b[l][None, :],
                    conv_w[l], ones_bd)
        flat = lambda a: a.reshape(-1, a.shape[-1])
        ya_l = _attention(q_l, za_l, [(k_l, v_l), (k_c, v_c)], diff_lambda[l], diff_subln_g[l][None, :],
                          l, tq=tq)
        x2 = _mixout(flat(yf_l), flat(yb_l), pl_[0], pl_[1], flat(ya_l), w_out_bf, x2, mod[l], lat_row,
                     post_g, post_prm, tm=tm_lat, seq_blocks=lat_blocks)
        if need_ctx_out:
            ya_c = _attention(q_c, za_c, [(k_c, v_c)], diff_lambda[l], diff_subln_g[l][None, :], l,
                              tq=min(tq, Tc))
            xc2 = _mixout(flat(yf_c), flat(yb_c), pc_[0], pc_[1], flat(ya_c), w_out_bf, xc2, mod[l],
                          ctx_row, post_g, post_prm, tm=tm_ctx, seq_blocks=ctx_blocks)
    return x2.reshape(B, T, D)
```

```python
import functools
import math

import jax
import jax.numpy as jnp
from jax import lax
from jax.experimental import pallas as pl
from jax.experimental.pallas import tpu as pltpu

F32 = jnp.float32
BF16 = jnp.bfloat16

D_MODEL = 1024
D_RWKV = 256
RWKV_HEAD = 64
LORA = 64
D_CONV = 256
CONV_WIDTH = 3
D_ATTN = 512
DIFF_HEAD_DIM = 64
DIFF_V_DIM = 128
DIFF_HEADS = 4
GRID_W = 64
ROPE_THETA = 10000.0
ROPE_AXIS_DIM = 32
NORM_EPS = 1e-6
RWKV_GN_EPS = 64e-5
D_IN = 4352
C_RWKV, C_CONV, C_Q, C_K, C_V, C_ZA = 0, 1280, 2304, 2816, 3328, 3840
N_RWKV = 1280
N_CONV = 1024

ATTN_ROWS = 128
CHUNK = 64
RWKV_BB = 4
PAIR = 128
VMEM_LIMIT = 56 * 1024 * 1024

NT_DIMS = (((1,), (1,)), ((), ()))
TN_DIMS = (((0,), (0,)), ((), ()))


def _cparams(sem):
    return pltpu.CompilerParams(dimension_semantics=sem, vmem_limit_bytes=VMEM_LIMIT)


def _silu(x):
    return x * jax.nn.sigmoid(x)


def _split2(x):
    hi = x.astype(BF16)
    lo = (x - hi.astype(F32)).astype(BF16)
    return hi, lo


def _dot_exact_rhs(x, m):
    hi, lo = _split2(x)
    return jnp.dot(hi, m, preferred_element_type=F32) + jnp.dot(lo, m, preferred_element_type=F32)


def _mod_kernel(c_ref, w_ref, b_ref, o_ref):
    a = _silu(c_ref[...])
    o_ref[0] = jnp.dot(a, w_ref[0], preferred_element_type=F32,
                       precision=lax.Precision.HIGHEST) + b_ref[0]


def _modulation(cond, mod_w, mod_b):
    L = mod_w.shape[0]
    R = cond.shape[0]
    tn = 512
    return pl.pallas_call(
        _mod_kernel,
        grid=(L, 3 * D_MODEL // tn),
        in_specs=[pl.BlockSpec((R, D_MODEL), lambda l, j: (0, 0)),
                  pl.BlockSpec((1, D_MODEL, tn), lambda l, j: (l, 0, j)),
                  pl.BlockSpec((1, 1, tn), lambda l, j: (l, 0, j))],
        out_specs=pl.BlockSpec((1, R, tn), lambda l, j: (l, 0, j)),
        out_shape=jax.ShapeDtypeStruct((L, R, 3 * D_MODEL), F32),
        compiler_params=_cparams(("parallel", "parallel")),
        name="modulation",
    )(cond, mod_w, mod_b.reshape(L, 1, 3 * D_MODEL))


def _rope(t, cosv, sinv):
    lane = lax.broadcasted_iota(jnp.int32, t.shape, 1)
    first_half = (lane % 32) < 16
    partner = jnp.where(first_half, pltpu.roll(t, 112, 1), pltpu.roll(t, 16, 1))
    return t * cosv + partner * sinv


def _inproj_kernel(x_ref, mod_ref, g_ref, w_ref, cos_ref, sin_ref,
                   prw_ref, pcv_ref, q_ref, k_ref, v_ref, za_ref, *, rope, groups):
    rg = x_ref.shape[0] // groups
    shift = mod_ref[0, 0:1, :]
    scale = mod_ref[0, 1:2, :]
    rs = [slice(g * rg, (g + 1) * rg) for g in range(groups)]
    hs = []
    for r in rs:
        xf = x_ref[r, :]
        ms = jnp.mean(xf * xf, axis=-1, keepdims=True)
        y = xf * lax.rsqrt(ms + NORM_EPS) * g_ref[...]
        hs.append((y * (1.0 + scale) + shift).astype(BF16))

    def proj(g, c0):
        return jnp.dot(hs[g], w_ref[:, c0:c0 + 256], preferred_element_type=F32)

    qk_scale = DIFF_HEAD_DIM ** -0.5 * math.log2(math.e)
    for g, r in enumerate(rs):
        for c in range(0, N_RWKV, 256):
            prw_ref[r, c:c + 256] = proj(g, C_RWKV + c).astype(BF16)
        for c in range(0, N_CONV, 256):
            pcv_ref[r, c:c + 256] = proj(g, C_CONV + c).astype(BF16)
        for c in range(0, D_ATTN, 256):
            v_ref[r, c:c + 256] = proj(g, C_V + c).astype(BF16)
            za_ref[r, c:c + 256] = proj(g, C_ZA + c).astype(BF16)
        if rope:
            cosv = cos_ref[r, :]
            sinv = sin_ref[r, :]
        for c in range(0, D_ATTN, 256):
            tq = proj(g, C_Q + c)
            tk = proj(g, C_K + c)
            for s in range(0, 256, 128):
                tqs = tq[:, s:s + 128]
                tks = tk[:, s:s + 128]
                if rope:
                    tqs = _rope(tqs, cosv, sinv)
                    tks = _rope(tks, cosv, sinv)
                q_ref[r, c + s:c + s + 128] = (tqs * qk_scale).astype(BF16)
                k_ref[r, c + s:c + s + 128] = tks.astype(BF16)


def _inproj(x2d, mod_l, mod_row_of_block, pre_g, w_in_bf, cos_t, sin_t, *, tm, rope, seq_blocks):
    M = x2d.shape[0]
    outs = [jax.ShapeDtypeStruct((M, n), BF16) for n in (N_RWKV, N_CONV, D_ATTN, D_ATTN, D_ATTN, D_ATTN)]
    row = lambda n: pl.BlockSpec((tm, n), lambda i: (i, 0))
    return pl.pallas_call(
        functools.partial(_inproj_kernel, rope=rope, groups=max(tm // 256, 1)),
        grid=(M // tm,),
        in_specs=[row(D_MODEL),
                  pl.BlockSpec((1, 3, D_MODEL), lambda i: (mod_row_of_block(i), 0, 0)),
                  pl.BlockSpec((1, D_MODEL), lambda i: (0, 0)),
                  pl.BlockSpec((D_MODEL, D_IN), lambda i: (0, 0), pipeline_mode=pl.Buffered(1)),
                  pl.BlockSpec((tm, 128), lambda i: (i % seq_blocks, 0)),
                  pl.BlockSpec((tm, 128), lambda i: (i % seq_blocks, 0))],
        out_specs=[row(N_RWKV), row(N_CONV), row(D_ATTN), row(D_ATTN), row(D_ATTN), row(D_ATTN)],
        out_shape=outs,
        compiler_params=_cparams(("parallel",)),
        name="inproj_rope" if rope else "inproj",
    )(x2d, mod_l, pre_g, w_in_bf, cos_t, sin_t)


def _tri_mask(reverse, inclusive):
    t = lax.broadcasted_iota(jnp.int32, (PAIR, PAIR), 0) % CHUNK
    s = lax.broadcasted_iota(jnp.int32, (PAIR, PAIR), 1) % CHUNK
    if reverse:
        return (s >= t) if inclusive else (s > t)
    return (s <= t) if inclusive else (s < t)


def _expand_pair(x, p):
    xs = x[:, PAIR * p:PAIR * (p + 1)]
    lane = lax.broadcasted_iota(jnp.int32, xs.shape, 1)
    zero = jnp.zeros_like(xs)
    return jnp.concatenate([jnp.where(lane < RWKV_HEAD, xs, zero),
                            jnp.where(lane >= RWKV_HEAD, xs, zero)], axis=0)


def _dot(a, b):
    return jnp.dot(a, b, preferred_element_type=F32)


def _rwkv_step(Xs, dirs, prm, Hs, with_output):
    w0, wup, a0, aup, k_k, k_a, ones_bd = prm
    n = len(Xs)
    rng = range(n)
    prs = [(i, p) for i in rng for p in range(2)]
    r = [Xs[i][:, 0:256].astype(F32) for i in rng]
    k = [Xs[i][:, 256:512].astype(F32) for i in rng]
    v = [Xs[i][:, 512:768] for i in rng]
    tlw = [jnp.tanh(Xs[i][:, 768:896].astype(F32)).astype(BF16) for i in rng]
    wraw = [w0[dirs[i]:dirs[i] + 1, :] + _dot(tlw[i], wup[dirs[i]]) for i in rng]
    a = [jax.nn.sigmoid(a0[dirs[i]:dirs[i] + 1, :] + _dot(Xs[i][:, 896:1024], aup[dirs[i]])) for i in rng]
    kkr = [k[i] * k_k for i in rng]
    ss = [_dot_exact_rhs(kkr[i] * kkr[i], ones_bd) for i in rng]
    wlog = [-jnp.exp(-jax.nn.softplus(-wraw[i]) - 0.5) for i in rng]
    ti = lax.broadcasted_iota(jnp.int32, (CHUNK, CHUNK), 0)
    si = lax.broadcasted_iota(jnp.int32, (CHUNK, CHUNK), 1)
    ltri = [jnp.where(si <= ti, 1.0, 0.0).astype(BF16), jnp.where(si >= ti, 1.0, 0.0).astype(BF16)]
    wsp = [_split2(wlog[i]) for i in rng]
    g = [_dot(ltri[dirs[i]], wsp[i][0]) + _dot(ltri[dirs[i]], wsp[i][1]) for i in rng]
    kk = [kkr[i] * lax.rsqrt(jnp.maximum(ss[i], 1e-24)) for i in rng]
    kmod = [k[i] * (1.0 + (a[i] - 1.0) * k_a) for i in rng]
    bb = [kk[i] * a[i] for i in rng]
    g_end = [g[i][0:1, :] if dirs[i] else g[i][CHUNK - 1:CHUNK, :] for i in rng]
    eneg = [jnp.exp(-g[i]) for i in rng]
    ec = [jnp.exp(g_end[i] - g[i]) for i in rng]
    at = [(-kk[i] * jnp.exp(g[i] - wlog[i])).astype(BF16) for i in rng]
    bt = [(bb[i] * eneg[i]).astype(BF16) for i in rng]
    kt = [(kmod[i] * eneg[i]).astype(BF16) for i in rng]
    bh = [(bb[i] * ec[i]).astype(BF16) for i in rng]
    kh = [(kmod[i] * ec[i]).astype(BF16) for i in rng]
    decay_end = [jnp.exp(g_end[i]) for i in rng]

    strict = [_tri_mask(False, False), _tri_mask(True, False)]
    eye = (lax.broadcasted_iota(jnp.int32, (PAIR, PAIR), 0)
           == lax.broadcasted_iota(jnp.int32, (PAIR, PAIR), 1))
    Pa = [_expand_pair(at[i], p) for i, p in prs]
    Vx = [_expand_pair(v[i], p) for i, p in prs]
    rhs = [jnp.concatenate([_expand_pair(bt[i], p), _expand_pair(kt[i], p)], axis=0) for i, p in prs]
    BK = [jnp.concatenate([_expand_pair(bh[i], p), _expand_pair(kh[i], p)], axis=0) for i, p in prs]
    m = range(len(prs))
    if with_output:
        incl = [_tri_mask(False, True), _tri_mask(True, True)]
        rt = [(r[i] * jnp.exp(g[i])).astype(BF16) for i in rng]
        Pr = [_expand_pair(rt[i], p) for i, p in prs]
        A4 = [lax.dot_general(jnp.concatenate([Pa[j], Pr[j]], axis=0), rhs[j], NT_DIMS,
                              preferred_element_type=F32) for j in m]
        Ar = [jnp.where(jnp.concatenate([incl[dirs[prs[j][0]]]] * 2, axis=1), A4[j][PAIR:, :], 0.0)
              .astype(BF16) for j in m]
    else:
        A4 = [lax.dot_general(Pa[j], rhs[j], NT_DIMS, preferred_element_type=F32) for j in m]
    Aab = [jnp.where(strict[dirs[prs[j][0]]], A4[j][:PAIR, :PAIR], 0.0) for j in m]
    Aak = [jnp.where(strict[dirs[prs[j][0]]], A4[j][:PAIR, PAIR:], 0.0).astype(BF16) for j in m]
    Tm = [jnp.where(eye, 1.0, 0.0) + Aab[j] for j in m]
    Pw = [Aab[j].astype(BF16) for j in m]
    AkV = [_dot(Aak[j], Vx[j]).astype(BF16) for j in m]
    for _ in range(5):
        Pw = [_dot(Pw[j], Pw[j]).astype(BF16) for j in m]
        Tm = [Tm[j] + _dot(Tm[j].astype(BF16), Pw[j]) for j in m]
    WU = [_dot(Tm[j].astype(BF16), jnp.concatenate([Pa[j], AkV[j]], axis=1)).astype(BF16)
          for j in m]
    R2 = [jnp.concatenate([WU[j], jnp.concatenate([jnp.zeros_like(Vx[j]), Vx[j]], axis=1)], axis=0)
          for j in m]
    MH = [lax.dot_general(BK[j], R2[j], TN_DIMS, preferred_element_type=F32) for j in m]
    Hb = [Hs[j].astype(BF16) for j in m]
    gcol = [jnp.broadcast_to(decay_end[i][:, PAIR * p:PAIR * (p + 1)], (PAIR, PAIR)).T for i, p in prs]
    new_Hs = [gcol[j] * Hs[j] + _dot(MH[j][:, :PAIR].astype(BF16), Hb[j]) + MH[j][:, PAIR:] for j in m]
    ys = None
    if with_output:
        QY = [_dot(Ar[j], R2[j]) for j in m]
        Qh = [(Pr[j].astype(F32) + QY[j][:, :PAIR]).astype(BF16) for j in m]
        yx = [_dot(Qh[j], Hb[j]) + QY[j][:, PAIR:] for j in m]
        yp = [yx[j][:CHUNK] + yx[j][CHUNK:] for j in m]
        ys = [jnp.concatenate([yp[2 * i], yp[2 * i + 1]], axis=1) for i in rng]
    return ys, new_Hs


def _rwkv_kernel(*refs, bb, with_output, has_init):
    it = iter(refs)
    pf_ref = next(it)
    pb_ref = next(it)
    w0_ref, wup_ref, a0_ref, aup_ref, kk_ref, ka_ref, ones_ref = (next(it) for _ in range(7))
    s0_ref = next(it) if has_init else None
    if with_output:
        yf_ref = next(it)
        yb_ref = next(it)
    st_ref = next(it)
    j = pl.program_id(1)

    @pl.when(j == 0)
    def _():
        if has_init:
            st_ref[...] = s0_ref[...]
        else:
            st_ref[...] = jnp.zeros_like(st_ref)

    prm = (w0_ref[...], wup_ref[...], a0_ref[...], aup_ref[...], kk_ref[...], ka_ref[...], ones_ref[...])
    inst = [(b, d) for b in range(bb) for d in range(2)]
    Xs = [(pb_ref if d else pf_ref)[b] for b, d in inst]
    Hs = [st_ref[b, 2 * d + p] for b, d in inst for p in range(2)]
    ys, new_Hs = _rwkv_step(Xs, [d for _, d in inst], prm, Hs, with_output)
    for i, (b, d) in enumerate(inst):
        st_ref[b, 2 * d] = new_Hs[2 * i]
        st_ref[b, 2 * d + 1] = new_Hs[2 * i + 1]
        if with_output:
            (yb_ref if d else yf_ref)[b] = ys[i].astype(BF16)


def _rwkv_scan(p_rwkv, prm, init_state, *, with_output, bb):
    B, T, _ = p_rwkv.shape
    assert B % bb == 0 and T % CHUNK == 0, (B, bb, T)
    n = T // CHUNK
    has_init = init_state is not None
    full = lambda a: pl.BlockSpec(a.shape, lambda i, j: (0,) * a.ndim)
    tok = lambda width, rev: pl.BlockSpec(
        (bb, CHUNK, width), (lambda i, j: (i, n - 1 - j, 0)) if rev else (lambda i, j: (i, j, 0)))
    st_spec = pl.BlockSpec((bb, 4, PAIR, PAIR), lambda i, j: (i, 0, 0, 0))
    in_specs = [tok(N_RWKV, False), tok(N_RWKV, True)] + [full(a) for a in prm]
    args = [p_rwkv, p_rwkv] + list(prm)
    if has_init:
        in_specs.append(st_spec)
        args.append(init_state)
    out_specs = []
    out_shape = []
    if with_output:
        out_specs += [tok(D_RWKV, False), tok(D_RWKV, True)]
        out_shape += [jax.ShapeDtypeStruct((B, T, D_RWKV), BF16)] * 2
    out_specs.append(st_spec)
    out_shape.append(jax.ShapeDtypeStruct((B, 4, PAIR, PAIR), F32))
    res = pl.pallas_call(
        functools.partial(_rwkv_kernel, bb=bb, with_output=with_output, has_init=has_init),
        grid=(B // bb, n),
        in_specs=in_specs,
        out_specs=out_specs,
        out_shape=out_shape,
        compiler_params=_cparams(("parallel", "arbitrary")),
        name="rwkv_scan" if with_output else "rwkv_state",
    )(*args)
    if with_output:
        return res[0], res[1], res[2]
    return None, None, res[0]


def _mixout_kernel(yf_ref, yb_ref, prw_ref, pcv_ref, pprev_ref, pnext_ref, ya_ref, w_ref, x_ref, mod_ref,
                   g_ref, a0_ref, aup_ref, ka_ref, rk_ref, lng_ref, lnb_ref, cw_ref, ones_ref, o_ref,
                   *, rows, seq_blocks):
    tm = x_ref.shape[0]
    seq_len = seq_blocks * tm
    t0 = (pl.program_id(0) % seq_blocks) * tm
    ones_bd = ones_ref[...]
    mean_bd = ones_bd * (1.0 / RWKV_HEAD)
    u_of = lambda ref: ref[:, 256:512].astype(F32) * ref[:, 512:768].astype(F32)
    u = jnp.concatenate([u_of(pprev_ref), u_of(pcv_ref), u_of(pnext_ref)], axis=0)
    n_u = tm + 16
    trow = lax.broadcasted_iota(jnp.int32, (n_u, D_CONV), 0) + (t0 - 8)
    u_prev = jnp.where(trow == 0, 0.0, pltpu.roll(u, 1, 0))
    u_next = jnp.where(trow == seq_len - 1, 0.0, pltpu.roll(u, n_u - 1, 0))
    conv = (u_prev * cw_ref[0:1, :] + u * cw_ref[1:2, :] + u_next * cw_ref[2:3, :])[8:8 + tm]
    rs = [slice(r0, r0 + rows) for r0 in range(0, tm, rows)]
    y_rwkv, y_conv = [], []
    for sl in rs:
        y = yf_ref[sl, :].astype(F32) + yb_ref[sl, :].astype(F32)
        mu = _dot_exact_rhs(y, mean_bd)
        yc = y - mu
        var = _dot_exact_rhs(yc * yc, mean_bd)
        yn = yc * lax.rsqrt(var + RWKV_GN_EPS) * lng_ref[...] + lnb_ref[...]
        r = prw_ref[sl, 0:256].astype(F32)
        k = prw_ref[sl, 256:512].astype(F32)
        v = prw_ref[sl, 512:768].astype(F32)
        la = prw_ref[sl, 896:1024]
        z = prw_ref[sl, 1024:1280].astype(F32)
        a_f = jax.nn.sigmoid(a0_ref[0:1, :] + _dot(la, aup_ref[0]))
        a_b = jax.nn.sigmoid(a0_ref[1:2, :] + _dot(la, aup_ref[1]))
        ksum = k * (2.0 + (a_f + a_b - 2.0) * ka_ref[...])
        bonus = _dot_exact_rhs(r * ksum * rk_ref[...], ones_bd) * v
        y_rwkv.append(((yn + bonus) * _silu(z)).astype(BF16))
        bg = pcv_ref[sl, 0:256].astype(F32)
        zc = pcv_ref[sl, 768:1024].astype(F32)
        y_conv.append((bg * conv[sl] * _silu(zc)).astype(BF16))
    accs = [_dot(y_rwkv[i], w_ref[0:D_RWKV, :]) + _dot(y_conv[i], w_ref[D_RWKV:D_RWKV + D_CONV, :])
            + _dot(ya_ref[sl, :], w_ref[D_RWKV + D_CONV:, :]) for i, sl in enumerate(rs)]
    for sl, acc in zip(rs, accs):
        ms = jnp.mean(acc * acc, axis=-1, keepdims=True)
        yn = acc * lax.rsqrt(ms + NORM_EPS) * g_ref[...]
        o_ref[sl, :] = x_ref[sl, :] + mod_ref[0, 2:3, :] * yn


def _mixout(yf, yb, p_rwkv, p_conv, ya, w_out_bf, x2d, mod_l, mod_row_of_block, post_g, prm, *, tm, seq_blocks):
    M = x2d.shape[0]
    hb = tm // 8
    row = lambda n: pl.BlockSpec((tm, n), lambda i: (i, 0))
    full = lambda a: pl.BlockSpec(a.shape, lambda i: (0,) * a.ndim)
    return pl.pallas_call(
        functools.partial(_mixout_kernel, rows=min(tm, 128), seq_blocks=seq_blocks),
        grid=(M // tm,),
        in_specs=[row(D_RWKV), row(D_RWKV), row(N_RWKV), row(N_CONV),
                  pl.BlockSpec((8, N_CONV), lambda i: (jnp.maximum(i * hb - 1, 0), 0)),
                  pl.BlockSpec((8, N_CONV), lambda i: (jnp.minimum((i + 1) * hb, M // 8 - 1), 0)),
                  row(D_ATTN),
                  pl.BlockSpec((D_MODEL, D_MODEL), lambda i: (0, 0), pipeline_mode=pl.Buffered(1)),
                  row(D_MODEL),
                  pl.BlockSpec((1, 3, D_MODEL), lambda i: (mod_row_of_block(i), 0, 0)),
                  pl.BlockSpec((1, D_MODEL), lambda i: (0, 0))] + [full(a) for a in prm],
        out_specs=row(D_MODEL),
        out_shape=jax.ShapeDtypeStruct((M, D_MODEL), F32),
        compiler_params=_cparams(("parallel",)),
        name="mixout",
    )(yf, yb, p_rwkv, p_conv, p_conv, p_conv, ya, w_out_bf, x2d, mod_l, post_g, *prm)


def _attn_kernel(*refs, n_kv, lam_init, rows):
    dl_ref, g_ref, q_ref, z_ref = refs[:4]
    kv_refs = refs[4:4 + 2 * n_kv]
    o_ref = refs[4 + 2 * n_kv]
    tq = q_ref.shape[1]
    ks = [kv_refs[2 * i][0] for i in range(n_kv)]
    vs = [kv_refs[2 * i + 1][0] for i in range(n_kv)]
    add = lambda a, b: a + b
    groups = [(r0, sub) for r0 in range(0, tq, rows) for sub in range(2)]
    lane = lax.broadcasted_iota(jnp.int32, (rows, DIFF_V_DIM), 1)
    sel = [lane < DIFF_HEAD_DIM, lane >= DIFF_HEAD_DIM]
    qm = [jnp.where(sel[sub], q_ref[0, r0:r0 + rows, :], 0) for r0, sub in groups]
    n = range(len(groups))
    s = [[lax.dot_general(qm[g], kk, NT_DIMS, preferred_element_type=F32) for kk in ks] for g in n]
    m = [functools.reduce(jnp.maximum, [jnp.max(x, axis=-1, keepdims=True) for x in s[g]]) for g in n]
    p = [[jnp.exp2(x - m[g]) for x in s[g]] for g in n]
    l = [functools.reduce(add, [jnp.sum(x, axis=-1, keepdims=True) for x in p[g]]) for g in n]
    o = [functools.reduce(add, [_dot(x.astype(BF16), vv) for x, vv in zip(p[g], vs)]) for g in n]
    dl = dl_ref[...]
    lam = (jnp.exp(jnp.sum(dl[0:1] * dl[1:2], axis=-1, keepdims=True))
           - jnp.exp(jnp.sum(dl[2:3] * dl[3:4], axis=-1, keepdims=True)) + lam_init)
    for i, r0 in enumerate(range(0, tq, rows)):
        od = o[2 * i] * (1.0 / l[2 * i]) - o[2 * i + 1] * (lam / l[2 * i + 1])
        ms = jnp.mean(od * od, axis=-1, keepdims=True)
        od = od * lax.rsqrt(ms + NORM_EPS) * g_ref[...] * (1.0 - lam_init)
        o_ref[0, r0:r0 + rows, :] = (od * _silu(z_ref[0, r0:r0 + rows, :].astype(F32))).astype(BF16)


def _attention(q, z, kvs, diff_lambda, subln_g, layer_idx, *, tq):
    B, Tq, _ = q.shape
    lam_init = 0.8 - 0.6 * math.exp(-0.3 * layer_idx)
    qspec = pl.BlockSpec((1, tq, DIFF_V_DIM), lambda b, h, i: (b, i, h))
    in_specs = [pl.BlockSpec(diff_lambda.shape, lambda b, h, i: (0, 0)),
                pl.BlockSpec((1, DIFF_V_DIM), lambda b, h, i: (0, 0)),
                qspec, qspec]
    args = [diff_lambda, subln_g, q, z]
    for k, v in kvs:
        Tk = k.shape[1]
        kvspec = pl.BlockSpec((1, Tk, DIFF_V_DIM), lambda b, h, i: (b, 0, h))
        in_specs += [kvspec, kvspec]
        args += [k, v]
    return pl.pallas_call(
        functools.partial(_attn_kernel, n_kv=len(kvs), lam_init=lam_init, rows=min(tq, ATTN_ROWS)),
        grid=(B, DIFF_HEADS, Tq // tq),
        in_specs=in_specs,
        out_specs=qspec,
        out_shape=jax.ShapeDtypeStruct((B, Tq, D_ATTN), BF16),
        compiler_params=_cparams(("parallel", "parallel", "arbitrary")),
        name="diff_attn",
    )(*args)


def _rope_tables(T):
    rows = T // GRID_W
    row = jnp.repeat(jnp.arange(rows, dtype=F32), GRID_W)
    col = jnp.tile(jnp.arange(GRID_W, dtype=F32), rows)
    inv_freq = ROPE_THETA ** (-jnp.arange(0, ROPE_AXIS_DIM, 2, dtype=F32) / ROPE_AXIS_DIM)
    ang_r = row[:, None] * inv_freq
    ang_c = col[:, None] * inv_freq
    cr, sr, cc, sc = jnp.cos(ang_r), jnp.sin(ang_r), jnp.cos(ang_c), jnp.sin(ang_c)
    cos64 = jnp.concatenate([cr, cr, cc, cc], axis=-1)
    sin64 = jnp.concatenate([-sr, sr, -sc, sc], axis=-1)
    return jnp.tile(cos64, (1, 2)), jnp.tile(sin64, (1, 2))


def _pad_lora(w_up):
    z = jnp.zeros_like(w_up[0])
    return jnp.stack([jnp.concatenate([w_up[0], z], axis=0),
                      jnp.concatenate([z, w_up[1]], axis=0)]).astype(BF16)


def kernel(x, c, ctx, c_ctx, mod_w, mod_b, norm_pre_g, norm_post_g, w_in, w_out, rwkv_w0, rwkv_w_up,
           rwkv_a0, rwkv_a_up, rwkv_k_k, rwkv_k_a, rwkv_r_k, rwkv_ln_g, rwkv_ln_b, conv_w, diff_lambda,
           diff_subln_g):
    B, T, D = x.shape
    Tc = ctx.shape[1]
    L = mod_w.shape[0]
    tm_lat = 1024 if T % 1024 == 0 else 256
    tm_ctx = 256
    tq = 512 if T % 512 == 0 else 256

    n_rows = ((B + 1 + 7) // 8) * 8
    cond = jnp.concatenate([c, c_ctx[None, :], jnp.zeros((n_rows - B - 1, D), F32)], axis=0)
    mod = _modulation(cond, mod_w, mod_b).reshape(L, n_rows, 3, D)

    cos_t, sin_t = _rope_tables(T)
    hid = lax.broadcasted_iota(jnp.int32, (D_RWKV, D_RWKV), 0) // RWKV_HEAD
    ones_bd = (hid == hid.T).astype(BF16)
    lat_blocks = T // tm_lat
    ctx_blocks = Tc // tm_ctx

    x2 = x.reshape(B * T, D)
    xc2 = ctx.reshape(B * Tc, D)
    for l in range(L):
        need_ctx_out = l < L - 1
        w_in_bf = w_in[l].astype(BF16)
        w_out_bf = w_out[l].astype(BF16)
        pre_g = norm_pre_g[l][None, :]
        post_g = norm_post_g[l][None, :]
        lat_row = lambda i: i // lat_blocks
        ctx_row = lambda i: B
        pl_ = _inproj(x2, mod[l], lat_row, pre_g, w_in_bf, cos_t, sin_t,
                      tm=tm_lat, rope=True, seq_blocks=lat_blocks)
        pc_ = _inproj(xc2, mod[l], ctx_row, pre_g, w_in_bf, cos_t, sin_t,
                      tm=tm_ctx, rope=False, seq_blocks=1)
        prw_l, pcv_l, q_l, k_l, v_l, za_l = [a.reshape(B, T, -1) for a in pl_]
        prw_c, pcv_c, q_c, k_c, v_c, za_c = [a.reshape(B, Tc, -1) for a in pc_]

        scan_prm = (rwkv_w0[l], _pad_lora(rwkv_w_up[l]), rwkv_a0[l], _pad_lora(rwkv_a_up[l]),
                    rwkv_k_k[l][None, :], rwkv_k_a[l][None, :], ones_bd)
        yf_c, yb_c, s_ctx = _rwkv_scan(prw_c, scan_prm, None, with_output=need_ctx_out, bb=RWKV_BB)
        yf_l, yb_l, _ = _rwkv_scan(prw_l, scan_prm, s_ctx, with_output=True, bb=RWKV_BB)

        post_prm = (rwkv_a0[l], _pad_lora(rwkv_a_up[l]), rwkv_k_a[l][None, :],
                    rwkv_r_k[l].reshape(1, D_RWKV), rwkv_ln_g[l][None, :], rwkv_ln_b[l][None, :],
                    conv_w[l], ones_bd)
        flat = lambda a: a.reshape(-1, a.shape[-1])
        ya_l = _attention(q_l, za_l, [(k_l, v_l), (k_c, v_c)], diff_lambda[l], diff_subln_g[l][None, :],
                          l, tq=tq)
        x2 = _mixout(flat(yf_l), flat(yb_l), pl_[0], pl_[1], flat(ya_l), w_out_bf, x2, mod[l], lat_row,
                     post_g, post_prm, tm=tm_lat, seq_blocks=lat_blocks)
        if need_ctx_out:
            ya_c = _attention(q_c, za_c, [(k_c, v_c)], diff_lambda[l], diff_subln_g[l][None, :], l,
                              tq=min(tq, Tc))
            xc2 = _mixout(flat(yf_c), flat(yb_c), pc_[0], pc_[1], flat(ya_c), w_out_bf, xc2, mod[l],
                          ctx_row, post_g, post_prm, tm=tm_ctx, seq_blocks=ctx_blocks)
    return x2.reshape(B, T, D)
```

```python
import functools
import math

import jax
import jax.numpy as jnp
from jax import lax
from jax.experimental import pallas as pl
from jax.experimental.pallas import tpu as pltpu

F32 = jnp.float32
BF16 = jnp.bfloat16

D_MODEL = 1024
D_RWKV = 256
RWKV_HEAD = 64
LORA = 64
D_CONV = 256
CONV_WIDTH = 3
D_ATTN = 512
DIFF_HEAD_DIM = 64
DIFF_V_DIM = 128
DIFF_HEADS = 4
GRID_W = 64
ROPE_THETA = 10000.0
ROPE_AXIS_DIM = 32
NORM_EPS = 1e-6
RWKV_GN_EPS = 64e-5
D_IN = 4352
C_RWKV, C_CONV, C_Q, C_K, C_V, C_ZA = 0, 1280, 2304, 2816, 3328, 3840
N_RWKV = 1280
N_CONV = 1024

ATTN_ROWS = 1024
CHUNK = 64
RWKV_BB = 4
PAIR = 128
VMEM_LIMIT = 56 * 1024 * 1024

NT_DIMS = (((1,), (1,)), ((), ()))
TN_DIMS = (((0,), (0,)), ((), ()))


def _cparams(sem):
    return pltpu.CompilerParams(dimension_semantics=sem, vmem_limit_bytes=VMEM_LIMIT)


def _silu(x):
    return x * jax.nn.sigmoid(x)


def _split2(x):
    hi = x.astype(BF16)
    lo = (x - hi.astype(F32)).astype(BF16)
    return hi, lo


def _dot_exact_rhs(x, m):
    hi, lo = _split2(x)
    return jnp.dot(hi, m, preferred_element_type=F32) + jnp.dot(lo, m, preferred_element_type=F32)


def _mod_kernel(c_ref, w_ref, b_ref, o_ref):
    a = _silu(c_ref[...])
    o_ref[0] = jnp.dot(a, w_ref[0], preferred_element_type=F32,
                       precision=lax.Precision.HIGHEST) + b_ref[0]


def _modulation(cond, mod_w, mod_b):
    L = mod_w.shape[0]
    R = cond.shape[0]
    tn = 512
    return pl.pallas_call(
        _mod_kernel,
        grid=(L, 3 * D_MODEL // tn),
        in_specs=[pl.BlockSpec((R, D_MODEL), lambda l, j: (0, 0)),
                  pl.BlockSpec((1, D_MODEL, tn), lambda l, j: (l, 0, j)),
                  pl.BlockSpec((1, 1, tn), lambda l, j: (l, 0, j))],
        out_specs=pl.BlockSpec((1, R, tn), lambda l, j: (l, 0, j)),
        out_shape=jax.ShapeDtypeStruct((L, R, 3 * D_MODEL), F32),
        compiler_params=_cparams(("parallel", "parallel")),
        name="modulation",
    )(cond, mod_w, mod_b.reshape(L, 1, 3 * D_MODEL))


def _rope(t, cosv, sinv):
    lane = lax.broadcasted_iota(jnp.int32, t.shape, 1)
    first_half = (lane % 32) < 16
    partner = jnp.where(first_half, pltpu.roll(t, 112, 1), pltpu.roll(t, 16, 1))
    return t * cosv + partner * sinv


def _inproj_kernel(x_ref, mod_ref, g_ref, w_ref, cos_ref, sin_ref,
                   prw_ref, pcv_ref, q_ref, k_ref, v_ref, za_ref, *, rope, groups):
    rg = x_ref.shape[0] // groups
    shift = mod_ref[0, 0:1, :]
    scale = mod_ref[0, 1:2, :]
    rs = [slice(g * rg, (g + 1) * rg) for g in range(groups)]
    hs = []
    for r in rs:
        xf = x_ref[r, :]
        ms = jnp.mean(xf * xf, axis=-1, keepdims=True)
        y = xf * lax.rsqrt(ms + NORM_EPS) * g_ref[...]
        hs.append((y * (1.0 + scale) + shift).astype(BF16))

    def proj(g, c0):
        return jnp.dot(hs[g], w_ref[:, c0:c0 + 256], preferred_element_type=F32)

    qk_scale = DIFF_HEAD_DIM ** -0.5 * math.log2(math.e)
    for g, r in enumerate(rs):
        for c in range(0, N_RWKV, 256):
            prw_ref[r, c:c + 256] = proj(g, C_RWKV + c).astype(BF16)
        for c in range(0, N_CONV, 256):
            pcv_ref[r, c:c + 256] = proj(g, C_CONV + c).astype(BF16)
        for c in range(0, D_ATTN, 256):
            v_ref[r, c:c + 256] = proj(g, C_V + c).astype(BF16)
            za_ref[r, c:c + 256] = proj(g, C_ZA + c).astype(BF16)
        if rope:
            cosv = cos_ref[r, :]
            sinv = sin_ref[r, :]
        for c in range(0, D_ATTN, 256):
            tq = proj(g, C_Q + c)
            tk = proj(g, C_K + c)
            for s in range(0, 256, 128):
                tqs = tq[:, s:s + 128]
                tks = tk[:, s:s + 128]
                if rope:
                    tqs = _rope(tqs, cosv, sinv)
                    tks = _rope(tks, cosv, sinv)
                q_ref[r, c + s:c + s + 128] = (tqs * qk_scale).astype(BF16)
                k_ref[r, c + s:c + s + 128] = tks.astype(BF16)


def _inproj(x2d, mod_l, mod_row_of_block, pre_g, w_in_bf, cos_t, sin_t, *, tm, rope, seq_blocks):
    M = x2d.shape[0]
    outs = [jax.ShapeDtypeStruct((M, n), BF16) for n in (N_RWKV, N_CONV, D_ATTN, D_ATTN, D_ATTN, D_ATTN)]
    row = lambda n: pl.BlockSpec((tm, n), lambda i: (i, 0))
    return pl.pallas_call(
        functools.partial(_inproj_kernel, rope=rope, groups=max(tm // 256, 1)),
        grid=(M // tm,),
        in_specs=[row(D_MODEL),
                  pl.BlockSpec((1, 3, D_MODEL), lambda i: (mod_row_of_block(i), 0, 0)),
                  pl.BlockSpec((1, D_MODEL), lambda i: (0, 0)),
                  pl.BlockSpec((D_MODEL, D_IN), lambda i: (0, 0), pipeline_mode=pl.Buffered(1)),
                  pl.BlockSpec((tm, 128), lambda i: (i % seq_blocks, 0)),
                  pl.BlockSpec((tm, 128), lambda i: (i % seq_blocks, 0))],
        out_specs=[row(N_RWKV), row(N_CONV), row(D_ATTN), row(D_ATTN), row(D_ATTN), row(D_ATTN)],
        out_shape=outs,
        compiler_params=_cparams(("parallel",)),
        name="inproj_rope" if rope else "inproj",
    )(x2d, mod_l, pre_g, w_in_bf, cos_t, sin_t)


def _tri_mask(reverse, inclusive):
    t = lax.broadcasted_iota(jnp.int32, (PAIR, PAIR), 0) % CHUNK
    s = lax.broadcasted_iota(jnp.int32, (PAIR, PAIR), 1) % CHUNK
    if reverse:
        return (s >= t) if inclusive else (s > t)
    return (s <= t) if inclusive else (s < t)


def _expand_pair(x, p):
    xs = x[:, PAIR * p:PAIR * (p + 1)]
    lane = lax.broadcasted_iota(jnp.int32, xs.shape, 1)
    zero = jnp.zeros_like(xs)
    return jnp.concatenate([jnp.where(lane < RWKV_HEAD, xs, zero),
                            jnp.where(lane >= RWKV_HEAD, xs, zero)], axis=0)


def _dot(a, b):
    return jnp.dot(a, b, preferred_element_type=F32)


def _rwkv_step(Xs, dirs, prm, Hs, with_output):
    w0, wup, a0, aup, k_k, k_a, ones_bd = prm
    n = len(Xs)
    rng = range(n)
    prs = [(i, p) for i in rng for p in range(2)]
    r = [Xs[i][:, 0:256].astype(F32) for i in rng]
    k = [Xs[i][:, 256:512].astype(F32) for i in rng]
    v = [Xs[i][:, 512:768] for i in rng]
    tlw = [jnp.tanh(Xs[i][:, 768:896].astype(F32)).astype(BF16) for i in rng]
    wraw = [w0[dirs[i]:dirs[i] + 1, :] + _dot(tlw[i], wup[dirs[i]]) for i in rng]
    a = [jax.nn.sigmoid(a0[dirs[i]:dirs[i] + 1, :] + _dot(Xs[i][:, 896:1024], aup[dirs[i]])) for i in rng]
    kkr = [k[i] * k_k for i in rng]
    ss = [_dot_exact_rhs(kkr[i] * kkr[i], ones_bd) for i in rng]
    wlog = [-jnp.exp(-jax.nn.softplus(-wraw[i]) - 0.5) for i in rng]
    ti = lax.broadcasted_iota(jnp.int32, (CHUNK, CHUNK), 0)
    si = lax.broadcasted_iota(jnp.int32, (CHUNK, CHUNK), 1)
    ltri = [jnp.where(si <= ti, 1.0, 0.0).astype(BF16), jnp.where(si >= ti, 1.0, 0.0).astype(BF16)]
    wsp = [_split2(wlog[i]) for i in rng]
    g = [_dot(ltri[dirs[i]], wsp[i][0]) + _dot(ltri[dirs[i]], wsp[i][1]) for i in rng]
    kk = [kkr[i] * lax.rsqrt(jnp.maximum(ss[i], 1e-24)) for i in rng]
    kmod = [k[i] * (1.0 + (a[i] - 1.0) * k_a) for i in rng]
    bb = [kk[i] * a[i] for i in rng]
    g_end = [g[i][0:1, :] if dirs[i] else g[i][CHUNK - 1:CHUNK, :] for i in rng]
    eneg = [jnp.exp(-g[i]) for i in rng]
    ec = [jnp.exp(g_end[i] - g[i]) for i in rng]
    at = [(-kk[i] * jnp.exp(g[i] - wlog[i])).astype(BF16) for i in rng]
    bt = [(bb[i] * eneg[i]).astype(BF16) for i in rng]
    kt = [(kmod[i] * eneg[i]).astype(BF16) for i in rng]
    bh = [(bb[i] * ec[i]).astype(BF16) for i in rng]
    kh = [(kmod[i] * ec[i]).astype(BF16) for i in rng]
    decay_end = [jnp.exp(g_end[i]) for i in rng]

    strict = [_tri_mask(False, False), _tri_mask(True, False)]
    eye = (lax.broadcasted_iota(jnp.int32, (PAIR, PAIR), 0)
           == lax.broadcasted_iota(jnp.int32, (PAIR, PAIR), 1))
    Pa = [_expand_pair(at[i], p) for i, p in prs]
    Vx = [_expand_pair(v[i], p) for i, p in prs]
    rhs = [jnp.concatenate([_expand_pair(bt[i], p), _expand_pair(kt[i], p)], axis=0) for i, p in prs]
    BK = [jnp.concatenate([_expand_pair(bh[i], p), _expand_pair(kh[i], p)], axis=0) for i, p in prs]
    m = range(len(prs))
    if with_output:
        incl = [_tri_mask(False, True), _tri_mask(True, True)]
        rt = [(r[i] * jnp.exp(g[i])).astype(BF16) for i in rng]
        Pr = [_expand_pair(rt[i], p) for i, p in prs]
        A4 = [lax.dot_general(jnp.concatenate([Pa[j], Pr[j]], axis=0), rhs[j], NT_DIMS,
                              preferred_element_type=F32) for j in m]
        Ar = [jnp.where(jnp.concatenate([incl[dirs[prs[j][0]]]] * 2, axis=1), A4[j][PAIR:, :], 0.0)
              .astype(BF16) for j in m]
    else:
        A4 = [lax.dot_general(Pa[j], rhs[j], NT_DIMS, preferred_element_type=F32) for j in m]
    Aab = [jnp.where(strict[dirs[prs[j][0]]], A4[j][:PAIR, :PAIR], 0.0) for j in m]
    Aak = [jnp.where(strict[dirs[prs[j][0]]], A4[j][:PAIR, PAIR:], 0.0).astype(BF16) for j in m]
    Tm = [jnp.where(eye, 1.0, 0.0) + Aab[j] for j in m]
    Pw = [Aab[j].astype(BF16) for j in m]
    AkV = [_dot(Aak[j], Vx[j]).astype(BF16) for j in m]
    for _ in range(5):
        Pw = [_dot(Pw[j], Pw[j]).astype(BF16) for j in m]
        Tm = [Tm[j] + _dot(Tm[j].astype(BF16), Pw[j]) for j in m]
    WU = [_dot(Tm[j].astype(BF16), jnp.concatenate([Pa[j], AkV[j]], axis=1)).astype(BF16)
          for j in m]
    R2 = [jnp.concatenate([WU[j], jnp.concatenate([jnp.zeros_like(Vx[j]), Vx[j]], axis=1)], axis=0)
          for j in m]
    MH = [lax.dot_general(BK[j], R2[j], TN_DIMS, preferred_element_type=F32) for j in m]
    Hb = [Hs[j].astype(BF16) for j in m]
    gcol = [jnp.broadcast_to(decay_end[i][:, PAIR * p:PAIR * (p + 1)], (PAIR, PAIR)).T for i, p in prs]
    new_Hs = [gcol[j] * Hs[j] + _dot(MH[j][:, :PAIR].astype(BF16), Hb[j]) + MH[j][:, PAIR:] for j in m]
    ys = None
    if with_output:
        QY = [_dot(Ar[j], R2[j]) for j in m]
        Qh = [(Pr[j].astype(F32) + QY[j][:, :PAIR]).astype(BF16) for j in m]
        yx = [_dot(Qh[j], Hb[j]) + QY[j][:, PAIR:] for j in m]
        yp = [yx[j][:CHUNK] + yx[j][CHUNK:] for j in m]
        ys = [jnp.concatenate([yp[2 * i], yp[2 * i + 1]], axis=1) for i in rng]
    return ys, new_Hs


def _rwkv_kernel(*refs, bb, with_output, has_init):
    it = iter(refs)
    pf_ref = next(it)
    pb_ref = next(it)
    w0_ref, wup_ref, a0_ref, aup_ref, kk_ref, ka_ref, ones_ref = (next(it) for _ in range(7))
    s0_ref = next(it) if has_init else None
    if with_output:
        yf_ref = next(it)
        yb_ref = next(it)
    st_ref = next(it)
    j = pl.program_id(1)

    @pl.when(j == 0)
    def _():
        if has_init:
            st_ref[...] = s0_ref[...]
        else:
            st_ref[...] = jnp.zeros_like(st_ref)

    prm = (w0_ref[...], wup_ref[...], a0_ref[...], aup_ref[...], kk_ref[...], ka_ref[...], ones_ref[...])
    inst = [(b, d) for b in range(bb) for d in range(2)]
    Xs = [(pb_ref if d else pf_ref)[b] for b, d in inst]
    Hs = [st_ref[b, 2 * d + p] for b, d in inst for p in range(2)]
    ys, new_Hs = _rwkv_step(Xs, [d for _, d in inst], prm, Hs, with_output)
    for i, (b, d) in enumerate(inst):
        st_ref[b, 2 * d] = new_Hs[2 * i]
        st_ref[b, 2 * d + 1] = new_Hs[2 * i + 1]
        if with_output:
            (yb_ref if d else yf_ref)[b] = ys[i].astype(BF16)


def _rwkv_scan(p_rwkv, prm, init_state, *, with_output, bb):
    B, T, _ = p_rwkv.shape
    assert B % bb == 0 and T % CHUNK == 0, (B, bb, T)
    n = T // CHUNK
    has_init = init_state is not None
    full = lambda a: pl.BlockSpec(a.shape, lambda i, j: (0,) * a.ndim)
    tok = lambda width, rev: pl.BlockSpec(
        (bb, CHUNK, width), (lambda i, j: (i, n - 1 - j, 0)) if rev else (lambda i, j: (i, j, 0)))
    st_spec = pl.BlockSpec((bb, 4, PAIR, PAIR), lambda i, j: (i, 0, 0, 0))
    in_specs = [tok(N_RWKV, False), tok(N_RWKV, True)] + [full(a) for a in prm]
    args = [p_rwkv, p_rwkv] + list(prm)
    if has_init:
        in_specs.append(st_spec)
        args.append(init_state)
    out_specs = []
    out_shape = []
    if with_output:
        out_specs += [tok(D_RWKV, False), tok(D_RWKV, True)]
        out_shape += [jax.ShapeDtypeStruct((B, T, D_RWKV), BF16)] * 2
    out_specs.append(st_spec)
    out_shape.append(jax.ShapeDtypeStruct((B, 4, PAIR, PAIR), F32))
    res = pl.pallas_call(
        functools.partial(_rwkv_kernel, bb=bb, with_output=with_output, has_init=has_init),
        grid=(B // bb, n),
        in_specs=in_specs,
        out_specs=out_specs,
        out_shape=out_shape,
        compiler_params=_cparams(("parallel", "arbitrary")),
        name="rwkv_scan" if with_output else "rwkv_state",
    )(*args)
    if with_output:
        return res[0], res[1], res[2]
    return None, None, res[0]


def _mixout_kernel(yf_ref, yb_ref, prw_ref, pcv_ref, pprev_ref, pnext_ref, ya_ref, w_ref, x_ref, mod_ref,
                   g_ref, a0_ref, aup_ref, ka_ref, rk_ref, lng_ref, lnb_ref, cw_ref, ones_ref, o_ref,
                   *, rows, seq_blocks):
    tm = x_ref.shape[0]
    seq_len = seq_blocks * tm
    t0 = (pl.program_id(0) % seq_blocks) * tm
    ones_bd = ones_ref[...]
    mean_bd = ones_bd * (1.0 / RWKV_HEAD)
    u_of = lambda ref: ref[:, 256:512].astype(F32) * ref[:, 512:768].astype(F32)
    u = jnp.concatenate([u_of(pprev_ref), u_of(pcv_ref), u_of(pnext_ref)], axis=0)
    n_u = tm + 16
    trow = lax.broadcasted_iota(jnp.int32, (n_u, D_CONV), 0) + (t0 - 8)
    u_prev = jnp.where(trow == 0, 0.0, pltpu.roll(u, 1, 0))
    u_next = jnp.where(trow == seq_len - 1, 0.0, pltpu.roll(u, n_u - 1, 0))
    conv = (u_prev * cw_ref[0:1, :] + u * cw_ref[1:2, :] + u_next * cw_ref[2:3, :])[8:8 + tm]
    rs = [slice(r0, r0 + rows) for r0 in range(0, tm, rows)]
    y_rwkv, y_conv = [], []
    for sl in rs:
        y = yf_ref[sl, :].astype(F32) + yb_ref[sl, :].astype(F32)
        mu = _dot_exact_rhs(y, mean_bd)
        yc = y - mu
        var = _dot_exact_rhs(yc * yc, mean_bd)
        yn = yc * lax.rsqrt(var + RWKV_GN_EPS) * lng_ref[...] + lnb_ref[...]
        r = prw_ref[sl, 0:256].astype(F32)
        k = prw_ref[sl, 256:512].astype(F32)
        v = prw_ref[sl, 512:768].astype(F32)
        la = prw_ref[sl, 896:1024]
        z = prw_ref[sl, 1024:1280].astype(F32)
        a_f = jax.nn.sigmoid(a0_ref[0:1, :] + _dot(la, aup_ref[0]))
        a_b = jax.nn.sigmoid(a0_ref[1:2, :] + _dot(la, aup_ref[1]))
        ksum = k * (2.0 + (a_f + a_b - 2.0) * ka_ref[...])
        bonus = _dot_exact_rhs(r * ksum * rk_ref[...], ones_bd) * v
        y_rwkv.append(((yn + bonus) * _silu(z)).astype(BF16))
        bg = pcv_ref[sl, 0:256].astype(F32)
        zc = pcv_ref[sl, 768:1024].astype(F32)
        y_conv.append((bg * conv[sl] * _silu(zc)).astype(BF16))
    accs = [_dot(y_rwkv[i], w_ref[0:D_RWKV, :]) + _dot(y_conv[i], w_ref[D_RWKV:D_RWKV + D_CONV, :])
            + _dot(ya_ref[sl, :], w_ref[D_RWKV + D_CONV:, :]) for i, sl in enumerate(rs)]
    for sl, acc in zip(rs, accs):
        ms = jnp.mean(acc * acc, axis=-1, keepdims=True)
        yn = acc * lax.rsqrt(ms + NORM_EPS) * g_ref[...]
        o_ref[sl, :] = x_ref[sl, :] + mod_ref[0, 2:3, :] * yn


def _mixout(yf, yb, p_rwkv, p_conv, ya, w_out_bf, x2d, mod_l, mod_row_of_block, post_g, prm, *, tm, seq_blocks):
    M = x2d.shape[0]
    hb = tm // 8
    row = lambda n: pl.BlockSpec((tm, n), lambda i: (i, 0))
    full = lambda a: pl.BlockSpec(a.shape, lambda i: (0,) * a.ndim)
    return pl.pallas_call(
        functools.partial(_mixout_kernel, rows=min(tm, 128), seq_blocks=seq_blocks),
        grid=(M // tm,),
        in_specs=[row(D_RWKV), row(D_RWKV), row(N_RWKV), row(N_CONV),
                  pl.BlockSpec((8, N_CONV), lambda i: (jnp.maximum(i * hb - 1, 0), 0)),
                  pl.BlockSpec((8, N_CONV), lambda i: (jnp.minimum((i + 1) * hb, M // 8 - 1), 0)),
                  row(D_ATTN),
                  pl.BlockSpec((D_MODEL, D_MODEL), lambda i: (0, 0), pipeline_mode=pl.Buffered(1)),
                  row(D_MODEL),
                  pl.BlockSpec((1, 3, D_MODEL), lambda i: (mod_row_of_block(i), 0, 0)),
                  pl.BlockSpec((1, D_MODEL), lambda i: (0, 0))] + [full(a) for a in prm],
        out_specs=row(D_MODEL),
        out_shape=jax.ShapeDtypeStruct((M, D_MODEL), F32),
        compiler_params=_cparams(("parallel",)),
        name="mixout",
    )(yf, yb, p_rwkv, p_conv, p_conv, p_conv, ya, w_out_bf, x2d, mod_l, post_g, *prm)


def _attn_kernel(*refs, n_kv, lam_init, rows):
    dl_ref, g_ref, q_ref, z_ref = refs[:4]
    kv_refs = refs[4:4 + 2 * n_kv]
    o_ref = refs[4 + 2 * n_kv]
    tq = q_ref.shape[1]
    ks = [kv_refs[2 * i][0] for i in range(n_kv)]
    vs = [kv_refs[2 * i + 1][0] for i in range(n_kv)]
    add = lambda a, b: a + b
    groups = [(r0, sub) for r0 in range(0, tq, rows) for sub in range(2)]
    lane = lax.broadcasted_iota(jnp.int32, (rows, DIFF_V_DIM), 1)
    sel = [lane < DIFF_HEAD_DIM, lane >= DIFF_HEAD_DIM]
    qm = [jnp.where(sel[sub], q_ref[0, r0:r0 + rows, :], 0) for r0, sub in groups]
    n = range(len(groups))
    s = [[lax.dot_general(qm[g], kk, NT_DIMS, preferred_element_type=F32) for kk in ks] for g in n]
    m = [functools.reduce(jnp.maximum, [jnp.max(x, axis=-1, keepdims=True) for x in s[g]]) for g in n]
    p = [[jnp.exp2(x - m[g]) for x in s[g]] for g in n]
    l = [functools.reduce(add, [jnp.sum(x, axis=-1, keepdims=True) for x in p[g]]) for g in n]
    o = [functools.reduce(add, [_dot(x.astype(BF16), vv) for x, vv in zip(p[g], vs)]) for g in n]
    dl = dl_ref[...]
    lam = (jnp.exp(jnp.sum(dl[0:1] * dl[1:2], axis=-1, keepdims=True))
           - jnp.exp(jnp.sum(dl[2:3] * dl[3:4], axis=-1, keepdims=True)) + lam_init)
    for i, r0 in enumerate(range(0, tq, rows)):
        od = o[2 * i] * (1.0 / l[2 * i]) - o[2 * i + 1] * (lam / l[2 * i + 1])
        ms = jnp.mean(od * od, axis=-1, keepdims=True)
        od = od * lax.rsqrt(ms + NORM_EPS) * g_ref[...] * (1.0 - lam_init)
        o_ref[0, r0:r0 + rows, :] = (od * _silu(z_ref[0, r0:r0 + rows, :].astype(F32))).astype(BF16)


def _attention(q, z, kvs, diff_lambda, subln_g, layer_idx, *, tq):
    B, Tq, _ = q.shape
    lam_init = 0.8 - 0.6 * math.exp(-0.3 * layer_idx)
    qspec = pl.BlockSpec((1, tq, DIFF_V_DIM), lambda b, h, i: (b, i, h))
    in_specs = [pl.BlockSpec(diff_lambda.shape, lambda b, h, i: (0, 0)),
                pl.BlockSpec((1, DIFF_V_DIM), lambda b, h, i: (0, 0)),
                qspec, qspec]
    args = [diff_lambda, subln_g, q, z]
    for k, v in kvs:
        Tk = k.shape[1]
        kvspec = pl.BlockSpec((1, Tk, DIFF_V_DIM), lambda b, h, i: (b, 0, h))
        in_specs += [kvspec, kvspec]
        args += [k, v]
    return pl.pallas_call(
        functools.partial(_attn_kernel, n_kv=len(kvs), lam_init=lam_init, rows=min(tq, ATTN_ROWS)),
        grid=(B, DIFF_HEADS, Tq // tq),
        in_specs=in_specs,
        out_specs=qspec,
        out_shape=jax.ShapeDtypeStruct((B, Tq, D_ATTN), BF16),
        compiler_params=_cparams(("parallel", "parallel", "arbitrary")),
        name="diff_attn",
    )(*args)


def _rope_tables(T):
    rows = T // GRID_W
    row = jnp.repeat(jnp.arange(rows, dtype=F32), GRID_W)
    col = jnp.tile(jnp.arange(GRID_W, dtype=F32), rows)
    inv_freq = ROPE_THETA ** (-jnp.arange(0, ROPE_AXIS_DIM, 2, dtype=F32) / ROPE_AXIS_DIM)
    ang_r = row[:, None] * inv_freq
    ang_c = col[:, None] * inv_freq
    cr, sr, cc, sc = jnp.cos(ang_r), jnp.sin(ang_r), jnp.cos(ang_c), jnp.sin(ang_c)
    cos64 = jnp.concatenate([cr, cr, cc, cc], axis=-1)
    sin64 = jnp.concatenate([-sr, sr, -sc, sc], axis=-1)
    return jnp.tile(cos64, (1, 2)), jnp.tile(sin64, (1, 2))


def _pad_lora(w_up):
    z = jnp.zeros_like(w_up[0])
    return jnp.stack([jnp.concatenate([w_up[0], z], axis=0),
                      jnp.concatenate([z, w_up[1]], axis=0)]).astype(BF16)


def kernel(x, c, ctx, c_ctx, mod_w, mod_b, norm_pre_g, norm_post_g, w_in, w_out, rwkv_w0, rwkv_w_up,
           rwkv_a0, rwkv_a_up, rwkv_k_k, rwkv_k_a, rwkv_r_k, rwkv_ln_g, rwkv_ln_b, conv_w, diff_lambda,
           diff_subln_g):
    B, T, D = x.shape
    Tc = ctx.shape[1]
    L = mod_w.shape[0]
    tm_lat = 1024 if T % 1024 == 0 else 256
    tm_ctx = 256
    tq = 1024 if T % 1024 == 0 else 256

    n_rows = ((B + 1 + 7) // 8) * 8
    cond = jnp.concatenate([c, c_ctx[None, :], jnp.zeros((n_rows - B - 1, D), F32)], axis=0)
    mod = _modulation(cond, mod_w, mod_b).reshape(L, n_rows, 3, D)

    cos_t, sin_t = _rope_tables(T)
    hid = lax.broadcasted_iota(jnp.int32, (D_RWKV, D_RWKV), 0) // RWKV_HEAD
    ones_bd = (hid == hid.T).astype(BF16)
    lat_blocks = T // tm_lat
    ctx_blocks = Tc // tm_ctx

    x2 = x.reshape(B * T, D)
    xc2 = ctx.reshape(B * Tc, D)
    for l in range(L):
        need_ctx_out = l < L - 1
        w_in_bf = w_in[l].astype(BF16)
        w_out_bf = w_out[l].astype(BF16)
        pre_g = norm_pre_g[l][None, :]
        post_g = norm_post_g[l][None, :]
        lat_row = lambda i: i // lat_blocks
        ctx_row = lambda i: B
        pl_ = _inproj(x2, mod[l], lat_row, pre_g, w_in_bf, cos_t, sin_t,
                      tm=tm_lat, rope=True, seq_blocks=lat_blocks)
        pc_ = _inproj(xc2, mod[l], ctx_row, pre_g, w_in_bf, cos_t, sin_t,
                      tm=tm_ctx, rope=False, seq_blocks=1)
        prw_l, pcv_l, q_l, k_l, v_l, za_l = [a.reshape(B, T, -1) for a in pl_]
        prw_c, pcv_c, q_c, k_c, v_c, za_c = [a.reshape(B, Tc, -1) for a in pc_]

        scan_prm = (rwkv_w0[l], _pad_lora(rwkv_w_up[l]), rwkv_a0[l], _pad_lora(rwkv_a_up[l]),
                    rwkv_k_k[l][None, :], rwkv_k_a[l][None, :], ones_bd)
        yf_c, yb_c, s_ctx = _rwkv_scan(prw_c, scan_prm, None, with_output=need_ctx_out, bb=RWKV_BB)
        yf_l, yb_l, _ = _rwkv_scan(prw_l, scan_prm, s_ctx, with_output=True, bb=RWKV_BB)

        post_prm = (rwkv_a0[l], _pad_lora(rwkv_a_up[l]), rwkv_k_a[l][None, :],
                    rwkv_r_k[l].reshape(1, D_RWKV), rwkv_ln_g[l][None, :], rwkv_ln_b[l][None, :],
                    conv_w[l], ones_bd)
        flat = lambda a: a.reshape(-1, a.shape[-1])
        ya_l = _attention(q_l, za_l, [(k_l, v_l), (k_c, v_c)], diff_lambda[l], diff_subln_g[l][None, :],
                          l, tq=tq)
        x2 = _mixout(flat(yf_l), flat(yb_l), pl_[0], pl_[1], flat(ya_l), w_out_bf, x2, mod[l], lat_row,
                     post_g, post_prm, tm=tm_lat, seq_blocks=lat_blocks)
        if need_ctx_out:
            ya_c = _attention(q_c, za_c, [(k_c, v_c)], diff_lambda[l], diff_subln_g[l][None, :], l,
                              tq=min(tq, Tc))
            xc2 = _mixout(flat(yf_c), flat(yb_c), pc_[0], pc_[1], flat(ya_c), w_out_bf, xc2, mod[l],
                          ctx_row, post_g, post_prm, tm=tm_ctx, seq_blocks=ctx_blocks)
    return x2.reshape(B, T, D)
```

```python
import functools
import math

import jax
import jax.numpy as jnp
from jax import lax
from jax.experimental import pallas as pl
from jax.experimental.pallas import tpu as pltpu

F32 = jnp.float32
BF16 = jnp.bfloat16

D_MODEL = 1024
D_RWKV = 256
RWKV_HEAD = 64
LORA = 64
D_CONV = 256
CONV_WIDTH = 3
D_ATTN = 512
DIFF_HEAD_DIM = 64
DIFF_V_DIM = 128
DIFF_HEADS = 4
GRID_W = 64
ROPE_THETA = 10000.0
ROPE_AXIS_DIM = 32
NORM_EPS = 1e-6
RWKV_GN_EPS = 64e-5
D_IN = 4352
C_RWKV, C_CONV, C_Q, C_K, C_V, C_ZA = 0, 1280, 2304, 2816, 3328, 3840
N_RWKV = 1280
N_CONV = 1024

ATTN_ROWS = 1024
ATTN_SMALL_SEQ = 256
CHUNK = 64
RWKV_BB = 4
PAIR = 128
VMEM_LIMIT = 56 * 1024 * 1024

NT_DIMS = (((1,), (1,)), ((), ()))
TN_DIMS = (((0,), (0,)), ((), ()))


def _cparams(sem):
    return pltpu.CompilerParams(dimension_semantics=sem, vmem_limit_bytes=VMEM_LIMIT)


def _silu(x):
    return x * jax.nn.sigmoid(x)


def _split2(x):
    hi = x.astype(BF16)
    lo = (x - hi.astype(F32)).astype(BF16)
    return hi, lo


def _dot_exact_rhs(x, m):
    hi, lo = _split2(x)
    return jnp.dot(hi, m, preferred_element_type=F32) + jnp.dot(lo, m, preferred_element_type=F32)


def _mod_kernel(c_ref, w_ref, b_ref, o_ref):
    a = _silu(c_ref[...])
    o_ref[0] = jnp.dot(a, w_ref[0], preferred_element_type=F32,
                       precision=lax.Precision.HIGHEST) + b_ref[0]


def _modulation(cond, mod_w, mod_b):
    L = mod_w.shape[0]
    R = cond.shape[0]
    tn = 512
    return pl.pallas_call(
        _mod_kernel,
        grid=(L, 3 * D_MODEL // tn),
        in_specs=[pl.BlockSpec((R, D_MODEL), lambda l, j: (0, 0)),
                  pl.BlockSpec((1, D_MODEL, tn), lambda l, j: (l, 0, j)),
                  pl.BlockSpec((1, 1, tn), lambda l, j: (l, 0, j))],
        out_specs=pl.BlockSpec((1, R, tn), lambda l, j: (l, 0, j)),
        out_shape=jax.ShapeDtypeStruct((L, R, 3 * D_MODEL), F32),
        compiler_params=_cparams(("parallel", "parallel")),
        name="modulation",
    )(cond, mod_w, mod_b.reshape(L, 1, 3 * D_MODEL))


def _rope(t, cosv, sinv):
    lane = lax.broadcasted_iota(jnp.int32, t.shape, 1)
    first_half = (lane % 32) < 16
    partner = jnp.where(first_half, pltpu.roll(t, 112, 1), pltpu.roll(t, 16, 1))
    return t * cosv + partner * sinv


def _inproj_kernel(x_ref, mod_ref, g_ref, w_ref, cos_ref, sin_ref,
                   prw_ref, pcv_ref, q_ref, k_ref, v_ref, za_ref, *, rope, groups):
    rg = x_ref.shape[0] // groups
    shift = mod_ref[0, 0:1, :]
    scale = mod_ref[0, 1:2, :]
    rs = [slice(g * rg, (g + 1) * rg) for g in range(groups)]
    hs = []
    for r in rs:
        xf = x_ref[r, :]
        ms = jnp.mean(xf * xf, axis=-1, keepdims=True)
        y = xf * lax.rsqrt(ms + NORM_EPS) * g_ref[...]
        hs.append((y * (1.0 + scale) + shift).astype(BF16))

    def proj(g, c0):
        return jnp.dot(hs[g], w_ref[:, c0:c0 + 256], preferred_element_type=F32)

    qk_scale = DIFF_HEAD_DIM ** -0.5 * math.log2(math.e)
    for g, r in enumerate(rs):
        for c in range(0, N_RWKV, 256):
            prw_ref[r, c:c + 256] = proj(g, C_RWKV + c).astype(BF16)
        for c in range(0, N_CONV, 256):
            pcv_ref[r, c:c + 256] = proj(g, C_CONV + c).astype(BF16)
        for c in range(0, D_ATTN, 256):
            v_ref[r, c:c + 256] = proj(g, C_V + c).astype(BF16)
            za_ref[r, c:c + 256] = proj(g, C_ZA + c).astype(BF16)
        if rope:
            cosv = cos_ref[r, :]
            sinv = sin_ref[r, :]
        for c in range(0, D_ATTN, 256):
            tq = proj(g, C_Q + c)
            tk = proj(g, C_K + c)
            for s in range(0, 256, 128):
                tqs = tq[:, s:s + 128]
                tks = tk[:, s:s + 128]
                if rope:
                    tqs = _rope(tqs, cosv, sinv)
                    tks = _rope(tks, cosv, sinv)
                q_ref[r, c + s:c + s + 128] = (tqs * qk_scale).astype(BF16)
                k_ref[r, c + s:c + s + 128] = tks.astype(BF16)


def _inproj(x2d, mod_l, mod_row_of_block, pre_g, w_in_bf, cos_t, sin_t, *, tm, rope, seq_blocks):
    M = x2d.shape[0]
    outs = [jax.ShapeDtypeStruct((M, n), BF16) for n in (N_RWKV, N_CONV, D_ATTN, D_ATTN, D_ATTN, D_ATTN)]
    row = lambda n: pl.BlockSpec((tm, n), lambda i: (i, 0))
    return pl.pallas_call(
        functools.partial(_inproj_kernel, rope=rope, groups=max(tm // 256, 1)),
        grid=(M // tm,),
        in_specs=[row(D_MODEL),
                  pl.BlockSpec((1, 3, D_MODEL), lambda i: (mod_row_of_block(i), 0, 0)),
                  pl.BlockSpec((1, D_MODEL), lambda i: (0, 0)),
                  pl.BlockSpec((D_MODEL, D_IN), lambda i: (0, 0), pipeline_mode=pl.Buffered(1)),
                  pl.BlockSpec((tm, 128), lambda i: (i % seq_blocks, 0)),
                  pl.BlockSpec((tm, 128), lambda i: (i % seq_blocks, 0))],
        out_specs=[row(N_RWKV), row(N_CONV), row(D_ATTN), row(D_ATTN), row(D_ATTN), row(D_ATTN)],
        out_shape=outs,
        compiler_params=_cparams(("parallel",)),
        name="inproj_rope" if rope else "inproj",
    )(x2d, mod_l, pre_g, w_in_bf, cos_t, sin_t)


def _tri_mask(reverse, inclusive):
    t = lax.broadcasted_iota(jnp.int32, (PAIR, PAIR), 0) % CHUNK
    s = lax.broadcasted_iota(jnp.int32, (PAIR, PAIR), 1) % CHUNK
    if reverse:
        return (s >= t) if inclusive else (s > t)
    return (s <= t) if inclusive else (s < t)


def _expand_pair(x, p):
    xs = x[:, PAIR * p:PAIR * (p + 1)]
    lane = lax.broadcasted_iota(jnp.int32, xs.shape, 1)
    zero = jnp.zeros_like(xs)
    return jnp.concatenate([jnp.where(lane < RWKV_HEAD, xs, zero),
                            jnp.where(lane >= RWKV_HEAD, xs, zero)], axis=0)


def _dot(a, b):
    return jnp.dot(a, b, preferred_element_type=F32)


def _rwkv_step(Xs, dirs, prm, Hs, with_output):
    w0, wup, a0, aup, k_k, k_a, ones_bd = prm
    n = len(Xs)
    rng = range(n)
    prs = [(i, p) for i in rng for p in range(2)]
    r = [Xs[i][:, 0:256].astype(F32) for i in rng]
    k = [Xs[i][:, 256:512].astype(F32) for i in rng]
    v = [Xs[i][:, 512:768] for i in rng]
    tlw = [jnp.tanh(Xs[i][:, 768:896].astype(F32)).astype(BF16) for i in rng]
    wraw = [w0[dirs[i]:dirs[i] + 1, :] + _dot(tlw[i], wup[dirs[i]]) for i in rng]
    a = [jax.nn.sigmoid(a0[dirs[i]:dirs[i] + 1, :] + _dot(Xs[i][:, 896:1024], aup[dirs[i]])) for i in rng]
    kkr = [k[i] * k_k for i in rng]
    ss = [_dot_exact_rhs(kkr[i] * kkr[i], ones_bd) for i in rng]
    wlog = [-jnp.exp(-jax.nn.softplus(-wraw[i]) - 0.5) for i in rng]
    ti = lax.broadcasted_iota(jnp.int32, (CHUNK, CHUNK), 0)
    si = lax.broadcasted_iota(jnp.int32, (CHUNK, CHUNK), 1)
    ltri = [jnp.where(si <= ti, 1.0, 0.0).astype(BF16), jnp.where(si >= ti, 1.0, 0.0).astype(BF16)]
    wsp = [_split2(wlog[i]) for i in rng]
    g = [_dot(ltri[dirs[i]], wsp[i][0]) + _dot(ltri[dirs[i]], wsp[i][1]) for i in rng]
    kk = [kkr[i] * lax.rsqrt(jnp.maximum(ss[i], 1e-24)) for i in rng]
    kmod = [k[i] * (1.0 + (a[i] - 1.0) * k_a) for i in rng]
    bb = [kk[i] * a[i] for i in rng]
    g_end = [g[i][0:1, :] if dirs[i] else g[i][CHUNK - 1:CHUNK, :] for i in rng]
    eneg = [jnp.exp(-g[i]) for i in rng]
    ec = [jnp.exp(g_end[i] - g[i]) for i in rng]
    at = [(-kk[i] * jnp.exp(g[i] - wlog[i])).astype(BF16) for i in rng]
    bt = [(bb[i] * eneg[i]).astype(BF16) for i in rng]
    kt = [(kmod[i] * eneg[i]).astype(BF16) for i in rng]
    bh = [(bb[i] * ec[i]).astype(BF16) for i in rng]
    kh = [(kmod[i] * ec[i]).astype(BF16) for i in rng]
    decay_end = [jnp.exp(g_end[i]) for i in rng]

    strict = [_tri_mask(False, False), _tri_mask(True, False)]
    eye = (lax.broadcasted_iota(jnp.int32, (PAIR, PAIR), 0)
           == lax.broadcasted_iota(jnp.int32, (PAIR, PAIR), 1))
    Pa = [_expand_pair(at[i], p) for i, p in prs]
    Vx = [_expand_pair(v[i], p) for i, p in prs]
    rhs = [jnp.concatenate([_expand_pair(bt[i], p), _expand_pair(kt[i], p)], axis=0) for i, p in prs]
    BK = [jnp.concatenate([_expand_pair(bh[i], p), _expand_pair(kh[i], p)], axis=0) for i, p in prs]
    m = range(len(prs))
    if with_output:
        incl = [_tri_mask(False, True), _tri_mask(True, True)]
        rt = [(r[i] * jnp.exp(g[i])).astype(BF16) for i in rng]
        Pr = [_expand_pair(rt[i], p) for i, p in prs]
        A4 = [lax.dot_general(jnp.concatenate([Pa[j], Pr[j]], axis=0), rhs[j], NT_DIMS,
                              preferred_element_type=F32) for j in m]
        Ar = [jnp.where(jnp.concatenate([incl[dirs[prs[j][0]]]] * 2, axis=1), A4[j][PAIR:, :], 0.0)
              .astype(BF16) for j in m]
    else:
        A4 = [lax.dot_general(Pa[j], rhs[j], NT_DIMS, preferred_element_type=F32) for j in m]
    Aab = [jnp.where(strict[dirs[prs[j][0]]], A4[j][:PAIR, :PAIR], 0.0) for j in m]
    Aak = [jnp.where(strict[dirs[prs[j][0]]], A4[j][:PAIR, PAIR:], 0.0).astype(BF16) for j in m]
    Tm = [jnp.where(eye, 1.0, 0.0) + Aab[j] for j in m]
    Pw = [Aab[j].astype(BF16) for j in m]
    AkV = [_dot(Aak[j], Vx[j]).astype(BF16) for j in m]
    for _ in range(5):
        Pw = [_dot(Pw[j], Pw[j]).astype(BF16) for j in m]
        Tm = [Tm[j] + _dot(Tm[j].astype(BF16), Pw[j]) for j in m]
    WU = [_dot(Tm[j].astype(BF16), jnp.concatenate([Pa[j], AkV[j]], axis=1)).astype(BF16)
          for j in m]
    R2 = [jnp.concatenate([WU[j], jnp.concatenate([jnp.zeros_like(Vx[j]), Vx[j]], axis=1)], axis=0)
          for j in m]
    MH = [lax.dot_general(BK[j], R2[j], TN_DIMS, preferred_element_type=F32) for j in m]
    Hb = [Hs[j].astype(BF16) for j in m]
    gcol = [jnp.broadcast_to(decay_end[i][:, PAIR * p:PAIR * (p + 1)], (PAIR, PAIR)).T for i, p in prs]
    new_Hs = [gcol[j] * Hs[j] + _dot(MH[j][:, :PAIR].astype(BF16), Hb[j]) + MH[j][:, PAIR:] for j in m]
    ys = None
    if with_output:
        QY = [_dot(Ar[j], R2[j]) for j in m]
        Qh = [(Pr[j].astype(F32) + QY[j][:, :PAIR]).astype(BF16) for j in m]
        yx = [_dot(Qh[j], Hb[j]) + QY[j][:, PAIR:] for j in m]
        yp = [yx[j][:CHUNK] + yx[j][CHUNK:] for j in m]
        ys = [jnp.concatenate([yp[2 * i], yp[2 * i + 1]], axis=1) for i in rng]
    return ys, new_Hs


def _rwkv_kernel(*refs, bb, with_output, has_init):
    it = iter(refs)
    pf_ref = next(it)
    pb_ref = next(it)
    w0_ref, wup_ref, a0_ref, aup_ref, kk_ref, ka_ref, ones_ref = (next(it) for _ in range(7))
    s0_ref = next(it) if has_init else None
    if with_output:
        yf_ref = next(it)
        yb_ref = next(it)
    st_ref = next(it)
    j = pl.program_id(1)

    @pl.when(j == 0)
    def _():
        if has_init:
            st_ref[...] = s0_ref[...]
        else:
            st_ref[...] = jnp.zeros_like(st_ref)

    prm = (w0_ref[...], wup_ref[...], a0_ref[...], aup_ref[...], kk_ref[...], ka_ref[...], ones_ref[...])
    inst = [(b, d) for b in range(bb) for d in range(2)]
    Xs = [(pb_ref if d else pf_ref)[b] for b, d in inst]
    Hs = [st_ref[b, 2 * d + p] for b, d in inst for p in range(2)]
    ys, new_Hs = _rwkv_step(Xs, [d for _, d in inst], prm, Hs, with_output)
    for i, (b, d) in enumerate(inst):
        st_ref[b, 2 * d] = new_Hs[2 * i]
        st_ref[b, 2 * d + 1] = new_Hs[2 * i + 1]
        if with_output:
            (yb_ref if d else yf_ref)[b] = ys[i].astype(BF16)


def _rwkv_scan(p_rwkv, prm, init_state, *, with_output, bb):
    B, T, _ = p_rwkv.shape
    assert B % bb == 0 and T % CHUNK == 0, (B, bb, T)
    n = T // CHUNK
    has_init = init_state is not None
    full = lambda a: pl.BlockSpec(a.shape, lambda i, j: (0,) * a.ndim)
    tok = lambda width, rev: pl.BlockSpec(
        (bb, CHUNK, width), (lambda i, j: (i, n - 1 - j, 0)) if rev else (lambda i, j: (i, j, 0)))
    st_spec = pl.BlockSpec((bb, 4, PAIR, PAIR), lambda i, j: (i, 0, 0, 0))
    in_specs = [tok(N_RWKV, False), tok(N_RWKV, True)] + [full(a) for a in prm]
    args = [p_rwkv, p_rwkv] + list(prm)
    if has_init:
        in_specs.append(st_spec)
        args.append(init_state)
    out_specs = []
    out_shape = []
    if with_output:
        out_specs += [tok(D_RWKV, False), tok(D_RWKV, True)]
        out_shape += [jax.ShapeDtypeStruct((B, T, D_RWKV), BF16)] * 2
    out_specs.append(st_spec)
    out_shape.append(jax.ShapeDtypeStruct((B, 4, PAIR, PAIR), F32))
    res = pl.pallas_call(
        functools.partial(_rwkv_kernel, bb=bb, with_output=with_output, has_init=has_init),
        grid=(B // bb, n),
        in_specs=in_specs,
        out_specs=out_specs,
        out_shape=out_shape,
        compiler_params=_cparams(("parallel", "arbitrary")),
        name="rwkv_scan" if with_output else "rwkv_state",
    )(*args)
    if with_output:
        return res[0], res[1], res[2]
    return None, None, res[0]


def _mixout_kernel(yf_ref, yb_ref, prw_ref, pcv_ref, pprev_ref, pnext_ref, ya_ref, w_ref, x_ref, mod_ref,
                   g_ref, a0_ref, aup_ref, ka_ref, rk_ref, lng_ref, lnb_ref, cw_ref, ones_ref, o_ref,
                   *, rows, seq_blocks):
    tm = x_ref.shape[0]
    seq_len = seq_blocks * tm
    t0 = (pl.program_id(0) % seq_blocks) * tm
    ones_bd = ones_ref[...]
    mean_bd = ones_bd * (1.0 / RWKV_HEAD)
    u_of = lambda ref: ref[:, 256:512].astype(F32) * ref[:, 512:768].astype(F32)
    u = jnp.concatenate([u_of(pprev_ref), u_of(pcv_ref), u_of(pnext_ref)], axis=0)
    n_u = tm + 16
    trow = lax.broadcasted_iota(jnp.int32, (n_u, D_CONV), 0) + (t0 - 8)
    u_prev = jnp.where(trow == 0, 0.0, pltpu.roll(u, 1, 0))
    u_next = jnp.where(trow == seq_len - 1, 0.0, pltpu.roll(u, n_u - 1, 0))
    conv = (u_prev * cw_ref[0:1, :] + u * cw_ref[1:2, :] + u_next * cw_ref[2:3, :])[8:8 + tm]
    rs = [slice(r0, r0 + rows) for r0 in range(0, tm, rows)]
    y_rwkv, y_conv = [], []
    for sl in rs:
        y = yf_ref[sl, :].astype(F32) + yb_ref[sl, :].astype(F32)
        mu = _dot_exact_rhs(y, mean_bd)
        yc = y - mu
        var = _dot_exact_rhs(yc * yc, mean_bd)
        yn = yc * lax.rsqrt(var + RWKV_GN_EPS) * lng_ref[...] + lnb_ref[...]
        r = prw_ref[sl, 0:256].astype(F32)
        k = prw_ref[sl, 256:512].astype(F32)
        v = prw_ref[sl, 512:768].astype(F32)
        la = prw_ref[sl, 896:1024]
        z = prw_ref[sl, 1024:1280].astype(F32)
        a_f = jax.nn.sigmoid(a0_ref[0:1, :] + _dot(la, aup_ref[0]))
        a_b = jax.nn.sigmoid(a0_ref[1:2, :] + _dot(la, aup_ref[1]))
        ksum = k * (2.0 + (a_f + a_b - 2.0) * ka_ref[...])
        bonus = _dot_exact_rhs(r * ksum * rk_ref[...], ones_bd) * v
        y_rwkv.append(((yn + bonus) * _silu(z)).astype(BF16))
        bg = pcv_ref[sl, 0:256].astype(F32)
        zc = pcv_ref[sl, 768:1024].astype(F32)
        y_conv.append((bg * conv[sl] * _silu(zc)).astype(BF16))
    accs = [_dot(y_rwkv[i], w_ref[0:D_RWKV, :]) + _dot(y_conv[i], w_ref[D_RWKV:D_RWKV + D_CONV, :])
            + _dot(ya_ref[sl, :], w_ref[D_RWKV + D_CONV:, :]) for i, sl in enumerate(rs)]
    for sl, acc in zip(rs, accs):
        ms = jnp.mean(acc * acc, axis=-1, keepdims=True)
        yn = acc * lax.rsqrt(ms + NORM_EPS) * g_ref[...]
        o_ref[sl, :] = x_ref[sl, :] + mod_ref[0, 2:3, :] * yn


def _mixout(yf, yb, p_rwkv, p_conv, ya, w_out_bf, x2d, mod_l, mod_row_of_block, post_g, prm, *, tm, seq_blocks):
    M = x2d.shape[0]
    hb = tm // 8
    row = lambda n: pl.BlockSpec((tm, n), lambda i: (i, 0))
    full = lambda a: pl.BlockSpec(a.shape, lambda i: (0,) * a.ndim)
    return pl.pallas_call(
        functools.partial(_mixout_kernel, rows=min(tm, 128), seq_blocks=seq_blocks),
        grid=(M // tm,),
        in_specs=[row(D_RWKV), row(D_RWKV), row(N_RWKV), row(N_CONV),
                  pl.BlockSpec((8, N_CONV), lambda i: (jnp.maximum(i * hb - 1, 0), 0)),
                  pl.BlockSpec((8, N_CONV), lambda i: (jnp.minimum((i + 1) * hb, M // 8 - 1), 0)),
                  row(D_ATTN),
                  pl.BlockSpec((D_MODEL, D_MODEL), lambda i: (0, 0), pipeline_mode=pl.Buffered(1)),
                  row(D_MODEL),
                  pl.BlockSpec((1, 3, D_MODEL), lambda i: (mod_row_of_block(i), 0, 0)),
                  pl.BlockSpec((1, D_MODEL), lambda i: (0, 0))] + [full(a) for a in prm],
        out_specs=row(D_MODEL),
        out_shape=jax.ShapeDtypeStruct((M, D_MODEL), F32),
        compiler_params=_cparams(("parallel",)),
        name="mixout",
    )(yf, yb, p_rwkv, p_conv, p_conv, p_conv, ya, w_out_bf, x2d, mod_l, post_g, *prm)


def _attn_kernel(*refs, n_kv, lam_init, rows):
    dl_ref, g_ref, q_ref, z_ref = refs[:4]
    kv_refs = refs[4:4 + 2 * n_kv]
    o_ref = refs[4 + 2 * n_kv]
    tq = q_ref.shape[1]
    heads = q_ref.shape[2] // DIFF_V_DIM
    hl = [slice(h * DIFF_V_DIM, (h + 1) * DIFF_V_DIM) for h in range(heads)]
    ks = [[kv_refs[2 * i][0, :, c] for i in range(n_kv)] for c in hl]
    vs = [[kv_refs[2 * i + 1][0, :, c] for i in range(n_kv)] for c in hl]
    add = lambda a, b: a + b
    blocks = [(h, r0) for h in range(heads) for r0 in range(0, tq, rows)]
    groups = [(h, r0, sub) for h, r0 in blocks for sub in range(2)]
    lane = lax.broadcasted_iota(jnp.int32, (rows, DIFF_V_DIM), 1)
    sel = [lane < DIFF_HEAD_DIM, lane >= DIFF_HEAD_DIM]
    qm = [jnp.where(sel[sub], q_ref[0, r0:r0 + rows, hl[h]], 0) for h, r0, sub in groups]
    n = range(len(groups))
    hg = [h for h, _, _ in groups]
    s = [[lax.dot_general(qm[g], kk, NT_DIMS, preferred_element_type=F32) for kk in ks[hg[g]]] for g in n]
    m = [functools.reduce(jnp.maximum, [jnp.max(x, axis=-1, keepdims=True) for x in s[g]]) for g in n]
    p = [[jnp.exp2(x - m[g]) for x in s[g]] for g in n]
    l = [functools.reduce(add, [jnp.sum(x, axis=-1, keepdims=True) for x in p[g]]) for g in n]
    o = [functools.reduce(add, [_dot(x.astype(BF16), vv) for x, vv in zip(p[g], vs[hg[g]])]) for g in n]
    dl = dl_ref[...]
    lam = (jnp.exp(jnp.sum(dl[0:1] * dl[1:2], axis=-1, keepdims=True))
           - jnp.exp(jnp.sum(dl[2:3] * dl[3:4], axis=-1, keepdims=True)) + lam_init)
    for i, (h, r0) in enumerate(blocks):
        od = o[2 * i] * (1.0 / l[2 * i]) - o[2 * i + 1] * (lam / l[2 * i + 1])
        ms = jnp.mean(od * od, axis=-1, keepdims=True)
        od = od * lax.rsqrt(ms + NORM_EPS) * g_ref[...] * (1.0 - lam_init)
        z = z_ref[0, r0:r0 + rows, hl[h]].astype(F32)
        o_ref[0, r0:r0 + rows, hl[h]] = (od * _silu(z)).astype(BF16)


def _attention(q, z, kvs, diff_lambda, subln_g, layer_idx, *, tq):
    B, Tq, _ = q.shape
    lam_init = 0.8 - 0.6 * math.exp(-0.3 * layer_idx)
    hp = DIFF_HEADS if Tq <= ATTN_SMALL_SEQ else 1
    width = hp * DIFF_V_DIM
    qspec = pl.BlockSpec((1, tq, width), lambda b, h, i: (b, i, h))
    in_specs = [pl.BlockSpec(diff_lambda.shape, lambda b, h, i: (0, 0)),
                pl.BlockSpec((1, DIFF_V_DIM), lambda b, h, i: (0, 0)),
                qspec, qspec]
    args = [diff_lambda, subln_g, q, z]
    for k, v in kvs:
        Tk = k.shape[1]
        kvspec = pl.BlockSpec((1, Tk, width), lambda b, h, i: (b, 0, h))
        in_specs += [kvspec, kvspec]
        args += [k, v]
    return pl.pallas_call(
        functools.partial(_attn_kernel, n_kv=len(kvs), lam_init=lam_init, rows=min(tq, ATTN_ROWS)),
        grid=(B, DIFF_HEADS // hp, Tq // tq),
        in_specs=in_specs,
        out_specs=qspec,
        out_shape=jax.ShapeDtypeStruct((B, Tq, D_ATTN), BF16),
        compiler_params=_cparams(("parallel", "parallel", "arbitrary")),
        name="diff_attn",
    )(*args)


def _rope_tables(T):
    rows = T // GRID_W
    row = jnp.repeat(jnp.arange(rows, dtype=F32), GRID_W)
    col = jnp.tile(jnp.arange(GRID_W, dtype=F32), rows)
    inv_freq = ROPE_THETA ** (-jnp.arange(0, ROPE_AXIS_DIM, 2, dtype=F32) / ROPE_AXIS_DIM)
    ang_r = row[:, None] * inv_freq
    ang_c = col[:, None] * inv_freq
    cr, sr, cc, sc = jnp.cos(ang_r), jnp.sin(ang_r), jnp.cos(ang_c), jnp.sin(ang_c)
    cos64 = jnp.concatenate([cr, cr, cc, cc], axis=-1)
    sin64 = jnp.concatenate([-sr, sr, -sc, sc], axis=-1)
    return jnp.tile(cos64, (1, 2)), jnp.tile(sin64, (1, 2))


def _pad_lora(w_up):
    z = jnp.zeros_like(w_up[0])
    return jnp.stack([jnp.concatenate([w_up[0], z], axis=0),
                      jnp.concatenate([z, w_up[1]], axis=0)]).astype(BF16)


def kernel(x, c, ctx, c_ctx, mod_w, mod_b, norm_pre_g, norm_post_g, w_in, w_out, rwkv_w0, rwkv_w_up,
           rwkv_a0, rwkv_a_up, rwkv_k_k, rwkv_k_a, rwkv_r_k, rwkv_ln_g, rwkv_ln_b, conv_w, diff_lambda,
           diff_subln_g):
    B, T, D = x.shape
    Tc = ctx.shape[1]
    L = mod_w.shape[0]
    tm_lat = 1024 if T % 1024 == 0 else 256
    tm_ctx = 256
    tq = 1024 if T % 1024 == 0 else 256

    n_rows = ((B + 1 + 7) // 8) * 8
    cond = jnp.concatenate([c, c_ctx[None, :], jnp.zeros((n_rows - B - 1, D), F32)], axis=0)
    mod = _modulation(cond, mod_w, mod_b).reshape(L, n_rows, 3, D)

    cos_t, sin_t = _rope_tables(T)
    hid = lax.broadcasted_iota(jnp.int32, (D_RWKV, D_RWKV), 0) // RWKV_HEAD
    ones_bd = (hid == hid.T).astype(BF16)
    lat_blocks = T // tm_lat
    ctx_blocks = Tc // tm_ctx

    x2 = x.reshape(B * T, D)
    xc2 = ctx.reshape(B * Tc, D)
    for l in range(L):
        need_ctx_out = l < L - 1
        w_in_bf = w_in[l].astype(BF16)
        w_out_bf = w_out[l].astype(BF16)
        pre_g = norm_pre_g[l][None, :]
        post_g = norm_post_g[l][None, :]
        lat_row = lambda i: i // lat_blocks
        ctx_row = lambda i: B
        pl_ = _inproj(x2, mod[l], lat_row, pre_g, w_in_bf, cos_t, sin_t,
                      tm=tm_lat, rope=True, seq_blocks=lat_blocks)
        pc_ = _inproj(xc2, mod[l], ctx_row, pre_g, w_in_bf, cos_t, sin_t,
                      tm=tm_ctx, rope=False, seq_blocks=1)
        prw_l, pcv_l, q_l, k_l, v_l, za_l = [a.reshape(B, T, -1) for a in pl_]
        prw_c, pcv_c, q_c, k_c, v_c, za_c = [a.reshape(B, Tc, -1) for a in pc_]

        scan_prm = (rwkv_w0[l], _pad_lora(rwkv_w_up[l]), rwkv_a0[l], _pad_lora(rwkv_a_up[l]),
                    rwkv_k_k[l][None, :], rwkv_k_a[l][None, :], ones_bd)
        yf_c, yb_c, s_ctx = _rwkv_scan(prw_c, scan_prm, None, with_output=need_ctx_out, bb=RWKV_BB)
        yf_l, yb_l, _ = _rwkv_scan(prw_l, scan_prm, s_ctx, with_output=True, bb=RWKV_BB)

        post_prm = (rwkv_a0[l], _pad_lora(rwkv_a_up[l]), rwkv_k_a[l][None, :],
                    rwkv_r_k[l].reshape(1, D_RWKV), rwkv_ln_g[l][None, :], rwkv_ln_b[l][None, :],
                    conv_w[l], ones_bd)
        flat = lambda a: a.reshape(-1, a.shape[-1])
        ya_l = _attention(q_l, za_l, [(k_l, v_l), (k_c, v_c)], diff_lambda[l], diff_subln_g[l][None, :],
                          l, tq=tq)
        x2 = _mixout(flat(yf_l), flat(yb_l), pl_[0], pl_[1], flat(ya_l), w_out_bf, x2, mod[l], lat_row,
                     post_g, post_prm, tm=tm_lat, seq_blocks=lat_blocks)
        if need_ctx_out:
            ya_c = _attention(q_c, za_c, [(k_c, v_c)], diff_lambda[l], diff_subln_g[l][None, :], l,
                              tq=min(tq, Tc))
            xc2 = _mixout(flat(yf_c), flat(yb_c), pc_[0], pc_[1], flat(ya_c), w_out_bf, xc2, mod[l],
                          ctx_row, post_g, post_prm, tm=tm_ctx, seq_blocks=ctx_blocks)
    return x2.reshape(B, T, D)
```
